```python
import math
import jax, jax.numpy as jnp
from jax import lax
import numpy as np

D_MODEL = 4096
BATCH = 32
SEQ = 256
DEPTH = 2
DEC_BATCH = 8
DEC_SEQ = 1024
PAST_LEN = 256

GRID_W = 64
Q_BLOCK = 128
ROPE_THETA = 10000.0
NORM_EPS = 1e-6
D_FF = 11008
N_MOD = 9
DIFF_HEADS = 8
DIFF_QK = 64
DIFF_V = 128
HEAD_DIM = 128
GQA_HEADS = 12
GQA_KV_HEADS = 4
GQA_GROUP = GQA_HEADS // GQA_KV_HEADS
MLA_HEADS = 12
MLA_NOPE = 128
MLA_ROPE = 64
MLA_V = 128
Q_LORA = 1024
KV_LORA = 512
MIX_WIDTH = DIFF_HEADS * DIFF_V + GQA_HEADS * HEAD_DIM + MLA_HEADS * MLA_V
IN_SIZES = (DIFF_HEADS * 2 * DIFF_QK, DIFF_HEADS * 2 * DIFF_QK, DIFF_HEADS * DIFF_V,
            GQA_HEADS * HEAD_DIM, GQA_KV_HEADS * HEAD_DIM, GQA_KV_HEADS * HEAD_DIM,
            Q_LORA, KV_LORA, MLA_ROPE)
IN_COLS = sum(IN_SIZES)
IN_SPLITS = tuple(int(s) for s in np.cumsum(IN_SIZES)[:-1])

kernel_name = "hymba_diff_gqa_mla_prefix_dit_step"


def rmsnorm(x, g):
    xf = x.astype(jnp.float32)
    y = xf * lax.rsqrt(jnp.mean(xf * xf, axis=-1, keepdims=True) + NORM_EPS)
    return (y * g.astype(jnp.float32)).astype(x.dtype)


def modulate(x, shift, scale):
    return x * (1 + scale) + shift


def swiglu(u, w_in, w_out):
    gate, up = jnp.split(u @ w_in, 2, axis=-1)
    return (jax.nn.silu(gate) * up) @ w_out


def rope_1d(x, pos):
    d = x.shape[-1]
    inv = ROPE_THETA ** (-(jnp.arange(0, d, 2, dtype=jnp.float32) / d))
    ang = pos[:, None] * inv[None, :]
    shape = (ang.shape[0],) + (1,) * (x.ndim - 3) + (ang.shape[1],)
    cos = jnp.cos(ang).reshape(shape)
    sin = jnp.sin(ang).reshape(shape)
    x1, x2 = jnp.split(x, 2, axis=-1)
    return jnp.concatenate([x1 * cos - x2 * sin, x1 * sin + x2 * cos], axis=-1)


def axial_rope(x, pos):
    rows, cols = pos
    half = x.shape[-1] // 2
    xf = x.astype(jnp.float32)
    out = jnp.concatenate([rope_1d(xf[..., :half], rows), rope_1d(xf[..., half:], cols)], axis=-1)
    return out.astype(x.dtype)


def sweep_query_blocks(fn, q):
    B, T = q.shape[:2]
    qb = q.reshape((B, T // Q_BLOCK, Q_BLOCK) + q.shape[2:])
    out = lax.map(fn, jnp.moveaxis(qb, 1, 0))
    out = jnp.moveaxis(out, 0, 1)
    return out.reshape((B, T) + out.shape[3:])


def softmax_attention(q, k, v):
    scale = q.shape[-1] ** -0.5

    def block(qb):
        s = jnp.einsum("bqhgd,bshd->bhgqs", qb, k, preferred_element_type=jnp.float32) * scale
        p = jax.nn.softmax(s, axis=-1).astype(v.dtype)
        return jnp.einsum("bhgqs,bshe->bqhge", p, v)

    return sweep_query_blocks(block, q)


def diff_attention(q, k, v, lam):
    scale = q.shape[-1] ** -0.5

    def block(qb):
        s = jnp.einsum("bqhcd,bshcd->bhcqs", qb, k, preferred_element_type=jnp.float32) * scale
        p = jax.nn.softmax(s, axis=-1)
        a = (p[:, :, 0] - lam * p[:, :, 1]).astype(v.dtype)
        return jnp.einsum("bhqs,bshe->bqhe", a, v)

    return sweep_query_blocks(block, q)


def project_mixers(u, lp, pos):
    B, T, _ = u.shape
    dq, dk, dv, gq, gk, gv, cq, ckv, kpe = jnp.split(u @ lp["w_in"], IN_SPLITS, axis=-1)
    dq = dq.reshape(B, T, DIFF_HEADS, 2, DIFF_QK)
    dk = dk.reshape(B, T, DIFF_HEADS, 2, DIFF_QK)
    dv = dv.reshape(B, T, DIFF_HEADS, DIFF_V)
    gq = rmsnorm(gq.reshape(B, T, GQA_HEADS, HEAD_DIM), lp["gqa_q_norm"])
    gk = rmsnorm(gk.reshape(B, T, GQA_KV_HEADS, HEAD_DIM), lp["gqa_k_norm"])
    gv = gv.reshape(B, T, GQA_KV_HEADS, HEAD_DIM)
    mq = (rmsnorm(cq, lp["mla_q_norm"]) @ lp["mla_w_uq"]).reshape(B, T, MLA_HEADS, MLA_NOPE + MLA_ROPE)
    mq_nope, mq_pe = mq[..., :MLA_NOPE], mq[..., MLA_NOPE:]
    ckv = rmsnorm(ckv, lp["mla_kv_norm"])
    if pos is not None:
        dq = axial_rope(dq, pos)
        dk = axial_rope(dk, pos)
        gq = axial_rope(gq, pos)
        gk = axial_rope(gk, pos)
        mq_pe = axial_rope(mq_pe, pos)
        kpe = axial_rope(kpe, pos)
    mq = jnp.concatenate([mq_nope, mq_pe], axis=-1)
    gq = gq.reshape(B, T, GQA_KV_HEADS, GQA_GROUP, HEAD_DIM)
    return (dq, gq, mq), (dk, dv, gk, gv, ckv, kpe)


def attend_mixers(queries, keys, lp, l):
    dq, gq, mq = queries
    dk, dv, gk, gv, ckv, kpe = keys
    B, T = dq.shape[:2]
    S = dk.shape[1]
    lam_init = 0.8 - 0.6 * math.exp(-0.3 * l)
    f32 = jnp.float32
    lam = (jnp.exp(jnp.sum(lp["diff_lq1"].astype(f32) * lp["diff_lk1"].astype(f32)))
           - jnp.exp(jnp.sum(lp["diff_lq2"].astype(f32) * lp["diff_lk2"].astype(f32))) + lam_init)
    o_diff = rmsnorm(diff_attention(dq, dk, dv, lam), lp["diff_subln"]) * (1.0 - lam_init)
    o_gqa = softmax_attention(gq, gk, gv)
    kv = (ckv @ lp["mla_w_ukv"]).reshape(B, S, MLA_HEADS, MLA_NOPE + MLA_V)
    k_nope, v_mla = kv[..., :MLA_NOPE], kv[..., MLA_NOPE:]
    k_mla = jnp.concatenate([k_nope, jnp.broadcast_to(kpe[:, :, None, :], (B, S, MLA_HEADS, MLA_ROPE))], axis=-1)
    o_mla = softmax_attention(mq[:, :, :, None, :], k_mla, v_mla)
    out = jnp.concatenate([o_diff.reshape(B, T, -1), o_gqa.reshape(B, T, -1), o_mla.reshape(B, T, -1)], axis=-1)
    return out @ lp["w_out"]


def trunk_layer(x, cvec, lp, l, pos, ctx):
    mod = jax.nn.silu(cvec) @ lp["w_mod"] + lp["b_mod"]
    sh1, sc1, g1, sh2, sc2, g2, sh3, sc3, g3 = jnp.split(mod[:, None, :], N_MOD, axis=-1)
    x = x + 0.5 * g1 * swiglu(modulate(rmsnorm(x, lp["norm_ffn1"]), sh1, sc1), lp["ffn1_w_in"], lp["ffn1_w_out"])
    u = modulate(rmsnorm(x, lp["norm_mix"]), sh2, sc2)
    queries, own = project_mixers(u, lp, pos)
    if ctx is None:
        keys = own
    else:
        keys = tuple(jnp.concatenate([kc, ko], axis=1) for kc, ko in zip(ctx, own))
    x = x + g2 * attend_mixers(queries, keys, lp, l)
    x = x + 0.5 * g3 * swiglu(modulate(rmsnorm(x, lp["norm_ffn2"]), sh3, sc3), lp["ffn2_w_in"], lp["ffn2_w_out"])
    return x, own


def setup_inputs(seed: int = 0) -> dict:
    key = jax.random.key(seed)
    ks = iter(jax.random.split(key, 40))
    D = D_MODEL

    def nrm(shape, scale):
        return jax.random.normal(next(ks), shape, jnp.float32) * scale

    def gain(shape):
        return 1.0 + nrm(shape, 0.05)

    return {
        "x_prompt": nrm((BATCH, SEQ, D), 1.0),
        "x_sample": nrm((DEC_BATCH, DEC_SEQ, D), 1.0),
        "cache_diff_k": nrm((DEC_BATCH, DEPTH, PAST_LEN, DIFF_HEADS, 2, DIFF_QK), 1.0),
        "cache_diff_v": nrm((DEC_BATCH, DEPTH, PAST_LEN, DIFF_HEADS, DIFF_V), 1.0),
        "cache_gqa_k": nrm((DEC_BATCH, DEPTH, PAST_LEN, GQA_KV_HEADS, HEAD_DIM), 1.0),
        "cache_gqa_v": nrm((DEC_BATCH, DEPTH, PAST_LEN, GQA_KV_HEADS, HEAD_DIM), 1.0),
        "cache_mla_ckv": nrm((DEC_BATCH, DEPTH, PAST_LEN, KV_LORA), 1.0),
        "cache_mla_kpe": nrm((DEC_BATCH, DEPTH, PAST_LEN, MLA_ROPE), 1.0),
        "c": nrm((DEC_BATCH, D), 1.0),
        "c_ctx": nrm((D,), 1.0),
        "w_mod": nrm((DEPTH, D, N_MOD * D), 0.5 * D ** -0.5),
        "b_mod": nrm((DEPTH, N_MOD * D), 0.02),
        "norm_ffn1": gain((DEPTH, D)),
        "norm_mix": gain((DEPTH, D)),
        "norm_ffn2": gain((DEPTH, D)),
        "ffn1_w_in": nrm((DEPTH, D, 2 * D_FF), D ** -0.5),
        "ffn1_w_out": nrm((DEPTH, D_FF, D), D_FF ** -0.5),
        "ffn2_w_in": nrm((DEPTH, D, 2 * D_FF), D ** -0.5),
        "ffn2_w_out": nrm((DEPTH, D_FF, D), D_FF ** -0.5),
        "w_in": nrm((DEPTH, D, IN_COLS), D ** -0.5),
        "w_out": nrm((DEPTH, MIX_WIDTH, D), MIX_WIDTH ** -0.5),
        "diff_lq1": nrm((DEPTH, DIFF_QK), 0.1),
        "diff_lk1": nrm((DEPTH, DIFF_QK), 0.1),
        "diff_lq2": nrm((DEPTH, DIFF_QK), 0.1),
        "diff_lk2": nrm((DEPTH, DIFF_QK), 0.1),
        "diff_subln": gain((DEPTH, DIFF_V)),
        "gqa_q_norm": gain((DEPTH, HEAD_DIM)),
        "gqa_k_norm": gain((DEPTH, HEAD_DIM)),
        "mla_q_norm": gain((DEPTH, Q_LORA)),
        "mla_kv_norm": gain((DEPTH, KV_LORA)),
        "mla_w_uq": nrm((DEPTH, Q_LORA, MLA_HEADS * (MLA_NOPE + MLA_ROPE)), Q_LORA ** -0.5),
        "mla_w_ukv": nrm((DEPTH, KV_LORA, MLA_HEADS * (MLA_NOPE + MLA_V)), KV_LORA ** -0.5),
        "final_norm": gain((D,)),
    }


def reference(x_prompt, x_sample, cache_diff_k, cache_diff_v, cache_gqa_k, cache_gqa_v, cache_mla_ckv,
              cache_mla_kpe, c, c_ctx, w_mod, b_mod, norm_ffn1, norm_mix, norm_ffn2, ffn1_w_in, ffn1_w_out,
              ffn2_w_in, ffn2_w_out, w_in, w_out, diff_lq1, diff_lk1, diff_lq2, diff_lk2, diff_subln,
              gqa_q_norm, gqa_k_norm, mla_q_norm, mla_kv_norm, mla_w_uq, mla_w_ukv, final_norm):
    def layer_params(l):
        return {
            "w_mod": w_mod[l], "b_mod": b_mod[l],
            "norm_ffn1": norm_ffn1[l], "norm_mix": norm_mix[l], "norm_ffn2": norm_ffn2[l],
            "ffn1_w_in": ffn1_w_in[l], "ffn1_w_out": ffn1_w_out[l],
            "ffn2_w_in": ffn2_w_in[l], "ffn2_w_out": ffn2_w_out[l],
            "w_in": w_in[l], "w_out": w_out[l],
            "diff_lq1": diff_lq1[l], "diff_lk1": diff_lk1[l], "diff_lq2": diff_lq2[l], "diff_lk2": diff_lk2[l],
            "diff_subln": diff_subln[l], "gqa_q_norm": gqa_q_norm[l], "gqa_k_norm": gqa_k_norm[l],
            "mla_q_norm": mla_q_norm[l], "mla_kv_norm": mla_kv_norm[l],
            "mla_w_uq": mla_w_uq[l], "mla_w_ukv": mla_w_ukv[l],
        }

    h = x_prompt
    ctx_states = [[] for _ in range(6)]
    for l in range(DEPTH):
        h, own = trunk_layer(h, c_ctx[None, :], layer_params(l), l, None, None)
        for i in range(6):
            ctx_states[i].append(own[i])
    y_prompt = rmsnorm(h, final_norm)

    T = x_sample.shape[1]
    ROWS = T // GRID_W
    rows = jnp.repeat(jnp.arange(ROWS, dtype=jnp.float32), GRID_W)
    cols = jnp.tile(jnp.arange(GRID_W, dtype=jnp.float32), ROWS)
    pos = (rows, cols)
    h = x_sample
    for l in range(DEPTH):
        ctx = (cache_diff_k[:, l], cache_diff_v[:, l], cache_gqa_k[:, l], cache_gqa_v[:, l],
               cache_mla_ckv[:, l], cache_mla_kpe[:, l])
        h, _ = trunk_layer(h, c, layer_params(l), l, pos, ctx)
    y_sample = rmsnorm(h, final_norm)

    new_diff_k = jnp.stack(ctx_states[0], axis=1)
    new_diff_v = jnp.stack(ctx_states[1], axis=1)
    new_gqa_k = jnp.stack(ctx_states[2], axis=1)
    new_gqa_v = jnp.stack(ctx_states[3], axis=1)
    new_mla_ckv = jnp.stack(ctx_states[4], axis=1)
    new_mla_kpe = jnp.stack(ctx_states[5], axis=1)
    return (y_prompt, y_sample, new_diff_k, new_diff_v, new_gqa_k, new_gqa_v, new_mla_ckv, new_mla_kpe)
```

```python
import functools
import math

import jax
import jax.numpy as jnp
from jax import lax
from jax.experimental import pallas as pl
from jax.experimental.pallas import tpu as pltpu

F32 = jnp.float32
BF16 = jnp.bfloat16

NORM_EPS = 1e-6
ROPE_THETA = 10000.0
GRID_W = 64
N_MOD = 9
DIFF_QK = 64
DIFF_V = 128
HEAD_DIM = 128
MLA_NOPE = 128
MLA_ROPE = 64
MLA_V = 128
LANES = 128
MOD_ROWS = 16
VMEM_LIMIT_BYTES = 56 * 1024 * 1024
W_TILE_BYTES_MAX = 8 * 1024 * 1024


def _params(sem):
    return pltpu.CompilerParams(dimension_semantics=sem, vmem_limit_bytes=VMEM_LIMIT_BYTES)


def _silu(x):
    return x / (1.0 + jnp.exp(-x))


def _rms(x, gain):
    y = x * lax.rsqrt(jnp.mean(x * x, axis=-1, keepdims=True) + NORM_EPS)
    return y * gain


def _softmax(s):
    m = jnp.max(s, axis=-1, keepdims=True)
    e = jnp.exp(s - m)
    return e / jnp.sum(e, axis=-1, keepdims=True)


def _rope(x, cos, sin_lo, sin_hi, half):
    return (x * cos + pltpu.roll(x, LANES - half, 1) * sin_lo + pltpu.roll(x, half, 1) * sin_hi)


_NT = (((1,), (1,)), ((), ()))


def _mod_kernel(c_ref, w_ref, b_ref, o_ref):
    a = _silu(c_ref[...]).astype(BF16)
    o_ref[...] = jnp.dot(a, w_ref[...].astype(BF16), preferred_element_type=F32) + b_ref[...]


def _modulation(cond, w_mod, b_mod):
    depth, d, n = w_mod.shape
    tn = 512 if n % 512 == 0 else n
    return pl.pallas_call(
        _mod_kernel,
        grid=(depth, n // tn),
        in_specs=[
            pl.BlockSpec((MOD_ROWS, d), lambda l, j: (0, 0)),
            pl.BlockSpec((None, d, tn), lambda l, j: (l, 0, j)),
            pl.BlockSpec((None, 1, tn), lambda l, j: (l, 0, j)),
        ],
        out_specs=pl.BlockSpec((None, MOD_ROWS, tn), lambda l, j: (l, 0, j)),
        out_shape=jax.ShapeDtypeStruct((depth, MOD_ROWS, n), F32),
        compiler_params=_params(("arbitrary", "arbitrary")),
        name="modulation",
    )(cond, w_mod, b_mod.reshape(depth, 1, n))


def _normmod_kernel(x_ref, g_ref, sh_ref, sc_ref, o_ref):
    y = _rms(x_ref[...], g_ref[...])
    o_ref[...] = (y * (1.0 + sc_ref[...]) + sh_ref[...]).astype(o_ref.dtype)


def _norm_kernel(x_ref, g_ref, o_ref):
    o_ref[...] = _rms(x_ref[...], g_ref[...]).astype(o_ref.dtype)


def _row_tile(m, target):
    t = min(m, target)
    assert m % t == 0
    return t


def _normmod(x, gain3, l, mod4, grp, chunk_shift, chunk_scale):
    m, d = x.shape
    g_base, g_rows = grp
    tr = _row_tile(g_rows, 256)
    return pl.pallas_call(
        _normmod_kernel,
        grid=(m // tr,),
        in_specs=[
            pl.BlockSpec((tr, d), lambda i: (i, 0)),
            pl.BlockSpec((None, 1, d), lambda i: (l, 0, 0)),
            pl.BlockSpec((None, None, 1, d), lambda i: (l, g_base + (i * tr) // g_rows, 0, chunk_shift)),
            pl.BlockSpec((None, None, 1, d), lambda i: (l, g_base + (i * tr) // g_rows, 0, chunk_scale)),
        ],
        out_specs=pl.BlockSpec((tr, d), lambda i: (i, 0)),
        out_shape=jax.ShapeDtypeStruct((m, d), BF16),
        compiler_params=_params(("arbitrary",)),
        name="normmod",
    )(x, gain3, mod4, mod4)


def _final_norm(x, gain):
    m, d = x.shape
    tr = _row_tile(m, 256)
    return pl.pallas_call(
        _norm_kernel,
        grid=(m // tr,),
        in_specs=[pl.BlockSpec((tr, d), lambda i: (i, 0)), pl.BlockSpec((1, d), lambda i: (0, 0))],
        out_specs=pl.BlockSpec((tr, d), lambda i: (i, 0)),
        out_shape=jax.ShapeDtypeStruct((m, d), F32),
        compiler_params=_params(("arbitrary",)),
        name="final_norm",
    )(x, gain.reshape(1, d))


def _mm_kernel(a_ref, w_ref, o_ref):
    a = a_ref[...].astype(BF16)
    o_ref[...] = jnp.dot(a, w_ref[...].astype(BF16), preferred_element_type=F32).astype(o_ref.dtype)


def _mm_swiglu_kernel(a_ref, wg_ref, wu_ref, o_ref):
    a = a_ref[...]
    g = jnp.dot(a, wg_ref[...].astype(BF16), preferred_element_type=F32)
    u = jnp.dot(a, wu_ref[...].astype(BF16), preferred_element_type=F32)
    o_ref[...] = (_silu(g) * u).astype(o_ref.dtype)


def _mm_resid_kernel(a_ref, w_ref, x_ref, gate_ref, o_ref, acc_ref, *, coef, nk, tk):
    k = pl.program_id(2)
    w = w_ref[...].astype(BF16)
    for kk in range(nk):
        @pl.when(k == kk)
        def _(kk=kk):
            part = jnp.dot(a_ref[:, kk * tk:(kk + 1) * tk], w, preferred_element_type=F32)
            if kk > 0:
                part = acc_ref[...] + part
            if kk == nk - 1:
                o_ref[...] = x_ref[...] + (coef * gate_ref[...]) * part
            else:
                acc_ref[...] = part


def _col_tile(n, target=256):
    return target if n % target == 0 else n


def _matmul(a, w, l, out_dtype, name, n=None):
    m, k = a.shape
    n = w.shape[-1] if n is None else n
    tm = _row_tile(m, 1024)
    tn = _col_tile(n)
    return pl.pallas_call(
        _mm_kernel,
        grid=(m // tm, n // tn),
        in_specs=[
            pl.BlockSpec((tm, k), lambda i, j: (i, 0)),
            pl.BlockSpec((None, k, tn), lambda i, j: (l, 0, j)),
        ],
        out_specs=pl.BlockSpec((tm, tn), lambda i, j: (i, j)),
        out_shape=jax.ShapeDtypeStruct((m, n), out_dtype),
        compiler_params=_params(("arbitrary", "arbitrary")),
        name=name,
    )(a, w)


def _matmul_cache(a4, l, w, lw, out_dtype, name):
    b, _, s, k = a4.shape
    n = w.shape[-1]
    tn = 256 if n % 256 == 0 else n
    return pl.pallas_call(
        _mm_kernel,
        grid=(b, n // tn),
        in_specs=[
            pl.BlockSpec((None, None, s, k), lambda i, j: (i, l, 0, 0)),
            pl.BlockSpec((None, k, tn), lambda i, j: (lw, 0, j)),
        ],
        out_specs=pl.BlockSpec((s, tn), lambda i, j: (i, j)),
        out_shape=jax.ShapeDtypeStruct((b * s, n), out_dtype),
        compiler_params=_params(("arbitrary", "arbitrary")),
        name=name,
    )(a4, w)


def _matmul_swiglu(a, w, l):
    m, k = a.shape
    f = w.shape[-1] // 2
    tm = _row_tile(m, 1024)
    tn = _col_tile(f)
    nj = f // tn
    return pl.pallas_call(
        _mm_swiglu_kernel,
        grid=(m // tm, nj),
        in_specs=[
            pl.BlockSpec((tm, k), lambda i, j: (i, 0)),
            pl.BlockSpec((None, k, tn), lambda i, j: (l, 0, j)),
            pl.BlockSpec((None, k, tn), lambda i, j: (l, 0, j + nj)),
        ],
        out_specs=pl.BlockSpec((tm, tn), lambda i, j: (i, j)),
        out_shape=jax.ShapeDtypeStruct((m, f), BF16),
        compiler_params=_params(("arbitrary", "arbitrary")),
        name="ffn_in",
    )(a, w, w)


def _matmul_resid(a, w, l, x, mod4, grp, chunk, coef, name):
    m, k = a.shape
    n = w.shape[-1]
    g_base, g_rows = grp
    tm = _row_tile(g_rows, 1024)
    tn = _col_tile(n)
    nk = 2 if (k * tn * 4 > W_TILE_BYTES_MAX and (k // 2) % LANES == 0) else 1
    tk = k // nk
    per_chunk = n // tn
    kern = functools.partial(_mm_resid_kernel, coef=coef, nk=nk, tk=tk)
    return pl.pallas_call(
        kern,
        grid=(m // tm, n // tn, nk),
        in_specs=[
            pl.BlockSpec((tm, k), lambda i, j, kk: (i, 0), pipeline_mode=pl.Buffered(1)),
            pl.BlockSpec((None, tk, tn), lambda i, j, kk: (l, kk, j)),
            pl.BlockSpec((tm, tn), lambda i, j, kk: (i, j)),
            pl.BlockSpec((None, None, 1, tn), lambda i, j, kk: (l, g_base + (i * tm) // g_rows, 0, chunk * per_chunk + j)),
        ],
        out_specs=pl.BlockSpec((tm, tn), lambda i, j, kk: (i, j)),
        out_shape=jax.ShapeDtypeStruct((m, n), F32),
        scratch_shapes=[pltpu.VMEM((tm, tn), F32)],
        compiler_params=_params(("arbitrary", "arbitrary", "arbitrary")),
        name=name,
    )(a, w, x, mod4)


def _post_kernel(*refs, dims, rope, emit_state):
    n_dqk, n_dv, n_gq, n_gk, q_lora, kv_lora = dims
    p_ref, pk_ref, gqn_ref, gkn_ref, mqn_ref, mkvn_ref = refs[:6]
    pos = 6
    if rope:
        c128, lo128, hi128, c64, lo64, hi64 = (r[...] for r in refs[pos:pos + 6])
        pos += 6
    qkv_ref, cq_ref, ckv_ref, kpe_ref = refs[pos:pos + 4]
    pos += 4
    if emit_state:
        gk_f_ref, ckv_f_ref = refs[pos:pos + 2]

    col = 0
    for _ in range(n_dqk // LANES):
        x = p_ref[:, col:col + LANES]
        if rope:
            x = _rope(x, c64, lo64, hi64, DIFF_QK // 4)
        qkv_ref[:, col:col + LANES] = x.astype(BF16)
        col += LANES
    qkv_ref[:, col:col + n_dv] = p_ref[:, col:col + n_dv].astype(BF16)
    col += n_dv
    for _ in range(n_gq // LANES):
        x = _rms(p_ref[:, col:col + LANES], gqn_ref[...])
        if rope:
            x = _rope(x, c128, lo128, hi128, HEAD_DIM // 4)
        qkv_ref[:, col:col + LANES] = x.astype(BF16)
        col += LANES
    for h in range(n_gk // LANES):
        x = _rms(p_ref[:, col:col + LANES], gkn_ref[...])
        if emit_state:
            gk_f_ref[:, h * LANES:(h + 1) * LANES] = x
        if rope:
            x = _rope(x, c128, lo128, hi128, HEAD_DIM // 4)
        qkv_ref[:, col:col + LANES] = x.astype(BF16)
        col += LANES
    qkv_ref[:, col:col + n_gk] = p_ref[:, col:col + n_gk].astype(BF16)
    col += n_gk
    cq_ref[...] = _rms(p_ref[:, col:col + q_lora], mqn_ref[...]).astype(BF16)
    col += q_lora
    ckv = _rms(p_ref[:, col:col + kv_lora], mkvn_ref[...])
    ckv_ref[...] = ckv.astype(BF16)
    if emit_state:
        ckv_f_ref[...] = ckv
    kpe = pk_ref[...]
    if rope:
        kpe = _rope(kpe, c64, lo64, hi64, MLA_ROPE // 4)
    kpe_ref[...] = kpe.astype(BF16)


def _post(p, pk, l, dims, gqn, gkn, mqn, mkvn, tables, t_len, emit_state):
    m, n = p.shape
    n_dqk, n_dv, n_gq, n_gk, q_lora, kv_lora = dims
    n_qkv = n_dqk + n_dv + n_gq + 2 * n_gk
    rope = tables is not None
    tr = _row_tile(t_len if rope else m, 256)
    in_specs = [
        pl.BlockSpec((tr, n), lambda i: (i, 0)),
        pl.BlockSpec((tr, LANES), lambda i: (i, 0)),
        pl.BlockSpec((None, 1, HEAD_DIM), lambda i: (l, 0, 0)),
        pl.BlockSpec((None, 1, HEAD_DIM), lambda i: (l, 0, 0)),
        pl.BlockSpec((None, 1, q_lora), lambda i: (l, 0, 0)),
        pl.BlockSpec((None, 1, kv_lora), lambda i: (l, 0, 0)),
    ]
    args = [p, pk, gqn, gkn, mqn, mkvn]
    if rope:
        nt = t_len // tr
        in_specs += [pl.BlockSpec((tr, LANES), lambda i: (i % nt, 0))] * 6
        args += list(tables)
    out_specs = [
        pl.BlockSpec((tr, n_qkv), lambda i: (i, 0)),
        pl.BlockSpec((tr, q_lora), lambda i: (i, 0)),
        pl.BlockSpec((tr, kv_lora), lambda i: (i, 0)),
        pl.BlockSpec((tr, LANES), lambda i: (i, 0)),
    ]
    out_shape = [
        jax.ShapeDtypeStruct((m, n_qkv), BF16),
        jax.ShapeDtypeStruct((m, q_lora), BF16),
        jax.ShapeDtypeStruct((m, kv_lora), BF16),
        jax.ShapeDtypeStruct((m, LANES), BF16),
    ]
    if emit_state:
        out_specs += [
            pl.BlockSpec((tr, n_gk), lambda i: (i, 0)),
            pl.BlockSpec((tr, kv_lora), lambda i: (i, 0)),
        ]
        out_shape += [
            jax.ShapeDtypeStruct((m, n_gk), F32),
            jax.ShapeDtypeStruct((m, kv_lora), F32),
        ]
    kern = functools.partial(_post_kernel, dims=dims, rope=rope, emit_state=emit_state)
    return pl.pallas_call(
        kern,
        grid=(m // tr,),
        in_specs=in_specs,
        out_specs=out_specs,
        out_shape=out_shape,
        compiler_params=_params(("arbitrary",)),
        name="mixer_post",
    )(*args)


def _stage(cat_ref, cache_ref, own_ref, s_cache):
    cat_ref[0:s_cache, :] = cache_ref[...].astype(BF16)
    cat_ref[s_cache:, :] = own_ref[...]


def _diff_attn_kernel(*refs, has_cache, s_cache, lam_init):
    q_ref, k_ref, v_ref = refs[:3]
    pos = 3
    if has_cache:
        kc_ref, vc_ref = refs[pos:pos + 2]
        pos += 2
    lq1, lk1, lq2, lk2, sub_ref, o_ref = refs[pos:pos + 6]
    pos += 6
    if has_cache:
        kcat, vcat = refs[pos:pos + 2]

        @pl.when(pl.program_id(2) == 0)
        def _():
            _stage(kcat, kc_ref, k_ref, s_cache)
            _stage(vcat, vc_ref, v_ref, s_cache)

        k, v = kcat[...], vcat[...]
    else:
        k, v = k_ref[...], v_ref[...]
    q = q_ref[...]
    lane = lax.broadcasted_iota(jnp.int32, q.shape, 1)
    zero = jnp.zeros_like(q)
    scale = DIFF_QK ** -0.5
    p0 = _softmax(lax.dot_general(jnp.where(lane < DIFF_QK, q, zero), k, _NT, preferred_element_type=F32) * scale)
    p1 = _softmax(lax.dot_general(jnp.where(lane >= DIFF_QK, q, zero), k, _NT, preferred_element_type=F32) * scale)
    lam = (jnp.exp(jnp.sum(lq1[...] * lk1[...], axis=-1, keepdims=True))
           - jnp.exp(jnp.sum(lq2[...] * lk2[...], axis=-1, keepdims=True)) + lam_init)
    a = (p0 - lam * p1).astype(BF16)
    o = jnp.dot(a, v, preferred_element_type=F32)
    o_ref[...] = (_rms(o, sub_ref[...]) * (1.0 - lam_init)).astype(o_ref.dtype)


def _gqa_attn_kernel(*refs, has_cache, s_cache, group):
    q_ref, k_ref, v_ref = refs[:3]
    pos = 3
    if has_cache:
        kc_ref, vc_ref = refs[pos:pos + 2]
        pos += 2
    o_ref = refs[pos]
    pos += 1
    if has_cache:
        kcat, vcat = refs[pos:pos + 2]

        @pl.when(pl.program_id(2) == 0)
        def _():
            _stage(kcat, kc_ref, k_ref, s_cache)
            _stage(vcat, vc_ref, v_ref, s_cache)

        k, v = kcat[...], vcat[...]
    else:
        k, v = k_ref[...], v_ref[...]
    scale = HEAD_DIM ** -0.5
    for g in range(group):
        q = q_ref[:, g * HEAD_DIM:(g + 1) * HEAD_DIM]
        p = _softmax(lax.dot_general(q, k, _NT, preferred_element_type=F32) * scale).astype(BF16)
        o_ref[:, g * HEAD_DIM:(g + 1) * HEAD_DIM] = jnp.dot(p, v, preferred_element_type=F32).astype(o_ref.dtype)


def _mla_attn_kernel(*refs, has_cache, s_cache, rope):
    qn_ref, qp_ref, kn_ref, v_ref, kp_ref = refs[:5]
    pos = 5
    if has_cache:
        knc_ref, vc_ref, kpc_ref = refs[pos:pos + 3]
        pos += 3
    if rope:
        c64, lo64, hi64 = (r[...] for r in refs[pos:pos + 3])
        pos += 3
    o_ref = refs[pos]
    pos += 1
    if has_cache:
        kcat, vcat, pcat = refs[pos:pos + 3]

        @pl.when(pl.program_id(2) == 0)
        def _():
            _stage(kcat, knc_ref, kn_ref, s_cache)
            _stage(vcat, vc_ref, v_ref, s_cache)
            _stage(pcat, kpc_ref, kp_ref, s_cache)

        kn, v, kp = kcat[...], vcat[...], pcat[...]
    else:
        kn, v, kp = kn_ref[...], v_ref[...], kp_ref[...]
    qp = qp_ref[...]
    if rope:
        qp = _rope(qp, c64, lo64, hi64, MLA_ROPE // 4)
    scale = (MLA_NOPE + MLA_ROPE) ** -0.5
    s = (lax.dot_general(qn_ref[...].astype(BF16), kn, _NT, preferred_element_type=F32)
         + lax.dot_general(qp.astype(BF16), kp, _NT, preferred_element_type=F32)) * scale
    p = _softmax(s).astype(BF16)
    o_ref[...] = jnp.dot(p, v, preferred_element_type=F32).astype(o_ref.dtype)


def _q_tile(t):
    return 256 if t % 256 == 0 else t


def _diff_attention(qkv, l, batch, t, heads, cache_k4, cache_v4, lparams, subln):
    has_cache = cache_k4 is not None
    tq = _q_tile(t)
    nq = t // tq
    s_cache = cache_k4.shape[2] if has_cache else 0
    in_specs = [
        pl.BlockSpec((tq, LANES), lambda b, h, i: (b * nq + i, h)),
        pl.BlockSpec((t, LANES), lambda b, h, i: (b, heads + h)),
        pl.BlockSpec((t, LANES), lambda b, h, i: (b, 2 * heads + h)),
    ]
    args = [qkv, qkv, qkv]
    scratch = []
    if has_cache:
        in_specs += [pl.BlockSpec((None, None, s_cache, LANES), lambda b, h, i: (b, l, 0, h))] * 2
        args += [cache_k4, cache_v4]
        scratch = [pltpu.VMEM((s_cache + t, LANES), BF16)] * 2
    in_specs += [pl.BlockSpec((None, 1, DIFF_QK), lambda b, h, i: (l, 0, 0))] * 4
    in_specs += [pl.BlockSpec((None, 1, DIFF_V), lambda b, h, i: (l, 0, 0))]
    args += list(lparams) + [subln]
    lam_init = 0.8 - 0.6 * math.exp(-0.3 * l)
    kern = functools.partial(_diff_attn_kernel, has_cache=has_cache, s_cache=s_cache, lam_init=lam_init)
    return pl.pallas_call(
        kern,
        grid=(batch, heads, nq),
        in_specs=in_specs,
        out_specs=pl.BlockSpec((tq, LANES), lambda b, h, i: (b * nq + i, h)),
        out_shape=jax.ShapeDtypeStruct((batch * t, heads * DIFF_V), BF16),
        scratch_shapes=scratch,
        compiler_params=_params(("arbitrary", "arbitrary", "arbitrary")),
        name="diff_attention",
    )(*args)


def _gqa_attention(qkv, l, batch, t, col0, q_heads, kv_heads, cache_k4, cache_v4):
    has_cache = cache_k4 is not None
    group = q_heads // kv_heads
    tq = _q_tile(t)
    nq = t // tq
    s_cache = cache_k4.shape[2] if has_cache else 0
    qw = group * HEAD_DIM
    assert col0 % qw == 0
    q0 = col0 // qw
    k0 = col0 // LANES + q_heads
    v0 = k0 + kv_heads
    in_specs = [
        pl.BlockSpec((tq, qw), lambda b, h, i: (b * nq + i, q0 + h)),
        pl.BlockSpec((t, LANES), lambda b, h, i: (b, k0 + h)),
        pl.BlockSpec((t, LANES), lambda b, h, i: (b, v0 + h)),
    ]
    args = [qkv, qkv, qkv]
    scratch = []
    if has_cache:
        in_specs += [pl.BlockSpec((None, None, s_cache, LANES), lambda b, h, i: (b, l, 0, h))] * 2
        args += [cache_k4, cache_v4]
        scratch = [pltpu.VMEM((s_cache + t, LANES), BF16)] * 2
    kern = functools.partial(_gqa_attn_kernel, has_cache=has_cache, s_cache=s_cache, group=group)
    return pl.pallas_call(
        kern,
        grid=(batch, kv_heads, nq),
        in_specs=in_specs,
        out_specs=pl.BlockSpec((tq, qw), lambda b, h, i: (b * nq + i, h)),
        out_shape=jax.ShapeDtypeStruct((batch * t, q_heads * HEAD_DIM), BF16),
        scratch_shapes=scratch,
        compiler_params=_params(("arbitrary", "arbitrary", "arbitrary")),
        name="gqa_attention",
    )(*args)


def _mla_attention(mq, kv, kpe, l, batch, t, heads, kv_cache, cache_kpe4, tables64):
    has_cache = kv_cache is not None
    rope = tables64 is not None
    tq = _q_tile(t)
    nq = t // tq
    s_cache = cache_kpe4.shape[2] if has_cache else 0
    in_specs = [
        pl.BlockSpec((tq, LANES), lambda b, h, i: (b * nq + i, h)),
        pl.BlockSpec((tq, LANES), lambda b, h, i: (b * nq + i, heads + h)),
        pl.BlockSpec((t, LANES), lambda b, h, i: (b, 2 * h)),
        pl.BlockSpec((t, LANES), lambda b, h, i: (b, 2 * h + 1)),
        pl.BlockSpec((t, LANES), lambda b, h, i: (b, 0)),
    ]
    args = [mq, mq, kv, kv, kpe]
    scratch = []
    if has_cache:
        in_specs += [
            pl.BlockSpec((s_cache, LANES), lambda b, h, i: (b, 2 * h)),
            pl.BlockSpec((s_cache, LANES), lambda b, h, i: (b, 2 * h + 1)),
            pl.BlockSpec((None, None, s_cache, LANES), lambda b, h, i: (b, l, 0, 0)),
        ]
        args += [kv_cache, kv_cache, cache_kpe4]
        scratch = [pltpu.VMEM((s_cache + t, LANES), BF16)] * 3
    if rope:
        in_specs += [pl.BlockSpec((tq, LANES), lambda b, h, i: (i, 0))] * 3
        args += list(tables64)
    kern = functools.partial(_mla_attn_kernel, has_cache=has_cache, s_cache=s_cache, rope=rope)
    return pl.pallas_call(
        kern,
        grid=(batch, heads, nq),
        in_specs=in_specs,
        out_specs=pl.BlockSpec((tq, LANES), lambda b, h, i: (b * nq + i, h)),
        out_shape=jax.ShapeDtypeStruct((batch * t, heads * MLA_V), BF16),
        scratch_shapes=scratch,
        compiler_params=_params(("arbitrary", "arbitrary", "arbitrary")),
        name="mla_attention",
    )(*args)


def _rope_tables(t_len, width):
    half = width // 2
    quarter = half // 2
    t = jnp.arange(t_len, dtype=jnp.int32)
    rows = (t // GRID_W).astype(F32)
    cols = (t % GRID_W).astype(F32)
    lane = jnp.arange(LANES, dtype=jnp.int32) % width
    inv = ROPE_THETA ** (-(jnp.arange(0, half, 2, dtype=F32) / half))
    inv_lane = inv[(lane % half) % quarter]
    pos = jnp.where((lane < half)[None, :], rows[:, None], cols[:, None])
    ang = pos * inv_lane[None, :]
    cos, sin = jnp.cos(ang), jnp.sin(ang)
    first = ((lane % half) < quarter)[None, :]
    zero = jnp.zeros_like(sin)
    return cos, jnp.where(first, -sin, zero), jnp.where(first, zero, sin)


def _layer(x, l, w, mod4, grp, batch, t, caches, tables, emit_state):
    ffn = lambda h, w_in, w_out, norm, c0, name: _matmul_resid(
        _matmul_swiglu(_normmod(h, norm, l, mod4, grp, c0, c0 + 1), w_in, l),
        w_out, l, h, mod4, grp, c0 + 2, 0.5, name)
    x = ffn(x, w["ffn1_w_in"], w["ffn1_w_out"], w["norm_ffn1"], 0, "ffn1_out")

    u = _normmod(x, w["norm_mix"], l, mod4, grp, 3, 4)
    dims = w["dims"]
    n_dqk, n_dv, n_gq, n_gk, q_lora, kv_lora = dims
    p = _matmul(u, w["w_in"], l, F32, "mixer_in", n=n_dqk + n_dv + n_gq + 2 * n_gk + q_lora + kv_lora)
    pk = _matmul(u, w["w_kpe"], l, F32, "mixer_in_kpe")
    post = _post(p, pk, l, dims, w["gqa_q_norm"], w["gqa_k_norm"], w["mla_q_norm"], w["mla_kv_norm"],
                 tables[0] + tables[1] if tables is not None else None, t, emit_state)
    qkv, cqn, ckvn, kpe = post[:4]

    diff_heads = n_dv // DIFF_V
    q_heads, kv_heads = n_gq // HEAD_DIM, n_gk // HEAD_DIM
    mla_heads = w["mla_w_ukv"].shape[-1] // (MLA_NOPE + MLA_V)
    has_cache = caches is not None
    o_diff = _diff_attention(qkv, l, batch, t, diff_heads,
                             caches["diff_k"] if has_cache else None, caches["diff_v"] if has_cache else None,
                             w["diff_l"], w["diff_subln"])
    o_gqa = _gqa_attention(qkv, l, batch, t, n_dqk + n_dv, q_heads, kv_heads,
                           caches["gqa_k"] if has_cache else None, caches["gqa_v"] if has_cache else None)
    mq = _matmul(cqn, w["mla_w_uq"], l, F32, "mla_q_up")
    kv = _matmul(ckvn, w["mla_w_ukv"], l, BF16, "mla_kv_up")
    kv_cache = _matmul_cache(caches["mla_ckv"], l, w["mla_w_ukv"], l, BF16, "mla_kv_up_cache") if has_cache else None
    o_mla = _mla_attention(mq, kv, kpe, l, batch, t, mla_heads, kv_cache,
                           caches["mla_kpe"] if has_cache else None, tables[1] if tables is not None else None)
    mixed = jnp.concatenate([o_diff, o_gqa, o_mla], axis=-1)
    x = _matmul_resid(mixed, w["w_out"], l, x, mod4, grp, 5, 1.0, "mixer_out")

    x = ffn(x, w["ffn2_w_in"], w["ffn2_w_out"], w["norm_ffn2"], 6, "ffn2_out")
    state = None
    if emit_state:
        gk_f, ckv_f = post[4:6]
        state = (p[:, n_dqk // 2:n_dqk], p[:, n_dqk:n_dqk + n_dv], gk_f,
                 p[:, n_dqk + n_dv + n_gq + n_gk:n_dqk + n_dv + n_gq + 2 * n_gk], ckv_f, pk[:, :MLA_ROPE])
    return x, state


def kernel(x_prompt, x_sample, cache_diff_k, cache_diff_v, cache_gqa_k, cache_gqa_v, cache_mla_ckv, cache_mla_kpe, c, c_ctx, w_mod, b_mod, norm_ffn1, norm_mix, norm_ffn2, ffn1_w_in, ffn1_w_out, ffn2_w_in, ffn2_w_out, w_in, w_out, diff_lq1, diff_lk1, diff_lq2, diff_lk2, diff_subln, gqa_q_norm, gqa_k_norm, mla_q_norm, mla_kv_norm, mla_w_uq, mla_w_ukv, final_norm):
    batch, seq, d = x_prompt.shape
    dec_batch, dec_seq, _ = x_sample.shape
    depth = w_mod.shape[0]
    past = cache_diff_k.shape[2]
    diff_heads = cache_diff_k.shape[3]
    kv_heads = cache_gqa_k.shape[3]
    q_lora, kv_lora = mla_q_norm.shape[-1], mla_kv_norm.shape[-1]
    mla_heads = mla_w_ukv.shape[-1] // (MLA_NOPE + MLA_V)
    n_dqk = 2 * diff_heads * 2 * DIFF_QK
    n_dv = diff_heads * DIFF_V
    n_gk = kv_heads * HEAD_DIM
    n_gq = w_in.shape[-1] - (n_dqk + n_dv + 2 * n_gk + q_lora + kv_lora + MLA_ROPE)
    dims = (n_dqk, n_dv, n_gq, n_gk, q_lora, kv_lora)
    assert dec_batch + 1 <= MOD_ROWS

    cond = jnp.zeros((MOD_ROWS, d), F32).at[0].set(c_ctx).at[1:1 + dec_batch].set(c)
    mod4 = _modulation(cond, w_mod, b_mod).reshape(depth, MOD_ROWS, 1, N_MOD * d)

    uq = mla_w_uq.reshape(depth, q_lora, mla_heads, MLA_NOPE + MLA_ROPE)
    uq_pe = jnp.pad(uq[..., MLA_NOPE:], ((0, 0), (0, 0), (0, 0), (0, LANES - MLA_ROPE)))
    uq = jnp.concatenate([uq[..., :MLA_NOPE].reshape(depth, q_lora, -1), uq_pe.reshape(depth, q_lora, -1)], axis=-1)

    w_kpe = jnp.pad(w_in[:, :, w_in.shape[-1] - MLA_ROPE:], ((0, 0), (0, 0), (0, LANES - MLA_ROPE)))
    cache_kpe = jnp.pad(cache_mla_kpe, ((0, 0), (0, 0), (0, 0), (0, LANES - MLA_ROPE)))

    r3 = lambda a: a.reshape(depth, 1, a.shape[-1])
    weights = {
        "w_kpe": w_kpe,
        "ffn1_w_in": ffn1_w_in, "ffn1_w_out": ffn1_w_out, "ffn2_w_in": ffn2_w_in, "ffn2_w_out": ffn2_w_out,
        "w_in": w_in, "w_out": w_out, "mla_w_uq": uq, "mla_w_ukv": mla_w_ukv,
        "norm_ffn1": r3(norm_ffn1), "norm_mix": r3(norm_mix), "norm_ffn2": r3(norm_ffn2),
        "gqa_q_norm": r3(gqa_q_norm), "gqa_k_norm": r3(gqa_k_norm),
        "mla_q_norm": r3(mla_q_norm), "mla_kv_norm": r3(mla_kv_norm),
        "diff_l": (r3(diff_lq1), r3(diff_lk1), r3(diff_lq2), r3(diff_lk2)), "diff_subln": r3(diff_subln),
        "dims": dims,
    }
    caches = {
        "diff_k": cache_diff_k.reshape(dec_batch, depth, past, -1),
        "diff_v": cache_diff_v.reshape(dec_batch, depth, past, -1),
        "gqa_k": cache_gqa_k.reshape(dec_batch, depth, past, -1),
        "gqa_v": cache_gqa_v.reshape(dec_batch, depth, past, -1),
        "mla_ckv": cache_mla_ckv, "mla_kpe": cache_kpe,
    }
    tables = (_rope_tables(dec_seq, HEAD_DIM), _rope_tables(dec_seq, DIFF_QK))

    h = x_prompt.reshape(batch * seq, d)
    states = []
    for l in range(depth):
        h, st = _layer(h, l, weights, mod4, (0, batch * seq), batch, seq, None, None, True)
        states.append(st)
    y_prompt = _final_norm(h, final_norm).reshape(batch, seq, d)

    h = x_sample.reshape(dec_batch * dec_seq, d)
    for l in range(depth):
        h, _ = _layer(h, l, weights, mod4, (1, dec_seq), dec_batch, dec_seq, caches, tables, False)
    y_sample = _final_norm(h, final_norm).reshape(dec_batch, dec_seq, d)

    stack = lambda i, shape: jnp.stack([s[i].reshape((batch, seq) + shape) for s in states], axis=1)
    return (y_prompt, y_sample,
            stack(0, (diff_heads, 2, DIFF_QK)), stack(1, (diff_heads, DIFF_V)),
            stack(2, (kv_heads, HEAD_DIM)), stack(3, (kv_heads, HEAD_DIM)),
            stack(4, (kv_lora,)), stack(5, (MLA_ROPE,)))
```

```python
import functools
import math
import operator

import jax
import jax.numpy as jnp
from jax import lax
from jax.experimental import pallas as pl
from jax.experimental.pallas import tpu as pltpu

F32 = jnp.float32
BF16 = jnp.bfloat16

NORM_EPS = 1e-6
ROPE_THETA = 10000.0
GRID_W = 64
N_MOD = 9
DIFF_QK = 64
DIFF_V = 128
HEAD_DIM = 128
MLA_NOPE = 128
MLA_ROPE = 64
MLA_V = 128
LANES = 128
MOD_ROWS = 16
VMEM_LIMIT_BYTES = 56 * 1024 * 1024
W_TILE_BYTES_MAX = 8 * 1024 * 1024
LOG2_E = 1.4426950408889634


def _params(sem):
    return pltpu.CompilerParams(dimension_semantics=sem, vmem_limit_bytes=VMEM_LIMIT_BYTES)


def _silu(x):
    return x / (1.0 + jnp.exp(-x))


def _rms(x, gain):
    y = x * lax.rsqrt(jnp.mean(x * x, axis=-1, keepdims=True) + NORM_EPS)
    return y * gain


def _rope(x, cos, sin_lo, sin_hi, shift):
    return (x * cos + pltpu.roll(x, LANES - shift, 1) * sin_lo + pltpu.roll(x, shift, 1) * sin_hi)


_NT = (((1,), (1,)), ((), ()))


def _dot_nt(a, b):
    return lax.dot_general(a, b, _NT, preferred_element_type=F32)


def _sum(xs):
    return functools.reduce(operator.add, xs)


def _mod_kernel(c_ref, w_ref, b_ref, o_ref):
    a = _silu(c_ref[...]).astype(BF16)
    o_ref[...] = jnp.dot(a, w_ref[...].astype(BF16), preferred_element_type=F32) + b_ref[...]


def _modulation(cond, w_mod, b_mod):
    depth, d, n = w_mod.shape
    tn = 512 if n % 512 == 0 else n
    return pl.pallas_call(
        _mod_kernel,
        grid=(depth, n // tn),
        in_specs=[
            pl.BlockSpec((MOD_ROWS, d), lambda l, j: (0, 0)),
            pl.BlockSpec((None, d, tn), lambda l, j: (l, 0, j)),
            pl.BlockSpec((None, 1, tn), lambda l, j: (l, 0, j)),
        ],
        out_specs=pl.BlockSpec((None, MOD_ROWS, tn), lambda l, j: (l, 0, j)),
        out_shape=jax.ShapeDtypeStruct((depth, MOD_ROWS, n), F32),
        compiler_params=_params(("arbitrary", "arbitrary")),
        name="modulation",
    )(cond, w_mod, b_mod.reshape(depth, 1, n))


def _normmod_kernel(x_ref, g_ref, sh_ref, sc_ref, o_ref):
    y = _rms(x_ref[...], g_ref[...])
    o_ref[...] = (y * (1.0 + sc_ref[...]) + sh_ref[...]).astype(o_ref.dtype)


def _norm_kernel(x_ref, g_ref, o_ref):
    o_ref[...] = _rms(x_ref[...], g_ref[...]).astype(o_ref.dtype)


def _row_tile(m, target):
    t = min(m, target)
    assert m % t == 0
    return t


def _normmod(x, gain3, l, mod4, grp, chunk_shift, chunk_scale):
    m, d = x.shape
    g_base, g_rows = grp
    tr = _row_tile(g_rows, 256)
    return pl.pallas_call(
        _normmod_kernel,
        grid=(m // tr,),
        in_specs=[
            pl.BlockSpec((tr, d), lambda i: (i, 0)),
            pl.BlockSpec((None, 1, d), lambda i: (l, 0, 0)),
            pl.BlockSpec((None, None, 1, d), lambda i: (l, g_base + (i * tr) // g_rows, 0, chunk_shift)),
            pl.BlockSpec((None, None, 1, d), lambda i: (l, g_base + (i * tr) // g_rows, 0, chunk_scale)),
        ],
        out_specs=pl.BlockSpec((tr, d), lambda i: (i, 0)),
        out_shape=jax.ShapeDtypeStruct((m, d), BF16),
        compiler_params=_params(("arbitrary",)),
        name="normmod",
    )(x, gain3, mod4, mod4)


def _final_norm(x, gain):
    m, d = x.shape
    tr = _row_tile(m, 256)
    return pl.pallas_call(
        _norm_kernel,
        grid=(m // tr,),
        in_specs=[pl.BlockSpec((tr, d), lambda i: (i, 0)), pl.BlockSpec((1, d), lambda i: (0, 0))],
        out_specs=pl.BlockSpec((tr, d), lambda i: (i, 0)),
        out_shape=jax.ShapeDtypeStruct((m, d), F32),
        compiler_params=_params(("arbitrary",)),
        name="final_norm",
    )(x, gain.reshape(1, d))


def _mm_kernel(a_ref, w_ref, o_ref):
    a = a_ref[...].astype(BF16)
    o_ref[...] = jnp.dot(a, w_ref[...].astype(BF16), preferred_element_type=F32).astype(o_ref.dtype)


def _mm_tail_kernel(a_ref, w_ref, o_ref, *, valid):
    w = w_ref[...]
    lane = lax.broadcasted_iota(jnp.int32, w.shape, 1)
    w = jnp.where(lane < valid, w, 0.0).astype(BF16)
    o_ref[...] = jnp.dot(a_ref[...], w, preferred_element_type=F32)


def _mm_swiglu_kernel(a_ref, wg_ref, wu_ref, o_ref):
    a = a_ref[...]
    g = jnp.dot(a, wg_ref[...].astype(BF16), preferred_element_type=F32)
    u = jnp.dot(a, wu_ref[...].astype(BF16), preferred_element_type=F32)
    o_ref[...] = (_silu(g) * u).astype(o_ref.dtype)


def _mm_resid_kernel(*refs, widths, coef, nk, tk):
    n_a = len(widths)
    a_refs = refs[:n_a]
    w_ref, x_ref, gate_ref, o_ref, acc_ref = refs[n_a:]
    if nk == 1:
        row = 0
        parts = []
        for a_ref, width in zip(a_refs, widths):
            parts.append(jnp.dot(a_ref[...], w_ref[row:row + width, :].astype(BF16), preferred_element_type=F32))
            row += width
        o_ref[...] = x_ref[...] + (coef * gate_ref[...]) * _sum(parts)
        return
    k = pl.program_id(2)
    a_ref, = a_refs
    for kk in range(nk):
        @pl.when(k == kk)
        def _(kk=kk):
            part = jnp.dot(a_ref[:, kk * tk:(kk + 1) * tk], w_ref[...].astype(BF16), preferred_element_type=F32)
            if kk > 0:
                part = acc_ref[...] + part
            if kk == nk - 1:
                o_ref[...] = x_ref[...] + (coef * gate_ref[...]) * part
            else:
                acc_ref[...] = part


def _col_tile(n, target=256):
    return target if n % target == 0 else n


def _matmul_tail(a, w, l, col0, name):
    m, k = a.shape
    valid = w.shape[-1] - col0
    assert col0 % LANES == 0 and 0 < valid <= LANES
    tm = _row_tile(m, 1024)
    return pl.pallas_call(
        functools.partial(_mm_tail_kernel, valid=valid),
        grid=(m // tm,),
        in_specs=[
            pl.BlockSpec((tm, k), lambda i: (i, 0)),
            pl.BlockSpec((None, k, LANES), lambda i: (l, 0, col0 // LANES)),
        ],
        out_specs=pl.BlockSpec((tm, LANES), lambda i: (i, 0)),
        out_shape=jax.ShapeDtypeStruct((m, LANES), F32),
        compiler_params=_params(("arbitrary",)),
        name=name,
    )(a, w)


def _matmul(a, w, l, out_dtype, name, n=None, tn_target=256):
    m, k = a.shape
    n = w.shape[-1] if n is None else n
    tm = _row_tile(m, 1024)
    tn = _col_tile(n, tn_target)
    return pl.pallas_call(
        _mm_kernel,
        grid=(m // tm, n // tn),
        in_specs=[
            pl.BlockSpec((tm, k), lambda i, j: (i, 0)),
            pl.BlockSpec((None, k, tn), lambda i, j: (l, 0, j)),
        ],
        out_specs=pl.BlockSpec((tm, tn), lambda i, j: (i, j)),
        out_shape=jax.ShapeDtypeStruct((m, n), out_dtype),
        compiler_params=_params(("arbitrary", "arbitrary")),
        name=name,
    )(a, w)


def _matmul_cache(a4, l, w, lw, out_dtype, name):
    b, _, s, k = a4.shape
    n = w.shape[-1]
    tn = _col_tile(n)
    return pl.pallas_call(
        _mm_kernel,
        grid=(b, n // tn),
        in_specs=[
            pl.BlockSpec((None, None, s, k), lambda i, j: (i, l, 0, 0)),
            pl.BlockSpec((None, k, tn), lambda i, j: (lw, 0, j)),
        ],
        out_specs=pl.BlockSpec((s, tn), lambda i, j: (i, j)),
        out_shape=jax.ShapeDtypeStruct((b * s, n), out_dtype),
        compiler_params=_params(("arbitrary", "arbitrary")),
        name=name,
    )(a4, w)


def _matmul_swiglu(a, w, l):
    m, k = a.shape
    f = w.shape[-1] // 2
    tm = _row_tile(m, 1024)
    tn = _col_tile(f)
    nj = f // tn
    return pl.pallas_call(
        _mm_swiglu_kernel,
        grid=(m // tm, nj),
        in_specs=[
            pl.BlockSpec((tm, k), lambda i, j: (i, 0)),
            pl.BlockSpec((None, k, tn), lambda i, j: (l, 0, j)),
            pl.BlockSpec((None, k, tn), lambda i, j: (l, 0, j + nj)),
        ],
        out_specs=pl.BlockSpec((tm, tn), lambda i, j: (i, j)),
        out_shape=jax.ShapeDtypeStruct((m, f), BF16),
        compiler_params=_params(("arbitrary", "arbitrary")),
        name="ffn_in",
    )(a, w, w)


def _matmul_resid(a_parts, w, l, x, mod4, grp, chunk, coef, name, tn_target=256):
    m = a_parts[0].shape[0]
    widths = tuple(a.shape[1] for a in a_parts)
    k = sum(widths)
    n = w.shape[-1]
    g_base, g_rows = grp
    tm = _row_tile(g_rows, 1024)
    tn = _col_tile(n, tn_target)
    nk = 2 if (len(a_parts) == 1 and k * tn * 4 > W_TILE_BYTES_MAX and (k // 2) % LANES == 0) else 1
    tk = k // nk
    per_chunk = n // tn
    kern = functools.partial(_mm_resid_kernel, widths=widths, coef=coef, nk=nk, tk=tk)
    a_mode = {"pipeline_mode": pl.Buffered(1)} if nk > 1 else {}
    return pl.pallas_call(
        kern,
        grid=(m // tm, n // tn, nk),
        in_specs=[pl.BlockSpec((tm, width), lambda i, j, kk: (i, 0), **a_mode) for width in widths] + [
            pl.BlockSpec((None, tk, tn), lambda i, j, kk: (l, kk, j)),
            pl.BlockSpec((tm, tn), lambda i, j, kk: (i, j)),
            pl.BlockSpec((None, None, 1, tn), lambda i, j, kk: (l, g_base + (i * tm) // g_rows, 0, chunk * per_chunk + j)),
        ],
        out_specs=pl.BlockSpec((tm, tn), lambda i, j, kk: (i, j)),
        out_shape=jax.ShapeDtypeStruct((m, n), F32),
        scratch_shapes=[pltpu.VMEM((tm, tn), F32)],
        compiler_params=_params(("arbitrary", "arbitrary", "arbitrary")),
        name=name,
    )(*a_parts, w, x, mod4)


def _post_kernel(*refs, dims, rope, emit_state):
    n_dqk, n_dv, n_gq, n_gk, q_lora, kv_lora = dims
    p_ref, pk_ref, gqn_ref, gkn_ref, mqn_ref, mkvn_ref = refs[:6]
    pos = 6
    if rope:
        c128, lo128, hi128, c64, lo64, hi64 = (r[...] for r in refs[pos:pos + 6])
        pos += 6
    qkv_ref, cq_ref, ckv_ref, kpe_ref = refs[pos:pos + 4]
    pos += 4
    if emit_state:
        dk_f_ref, dv_f_ref, gk_f_ref, gv_f_ref, ckv_f_ref = refs[pos:pos + 5]

    col = 0
    for _ in range(n_dqk // LANES):
        x = p_ref[:, col:col + LANES]
        if rope:
            x = _rope(x, c64, lo64, hi64, DIFF_QK // 4)
        qkv_ref[:, col:col + LANES] = x.astype(BF16)
        col += LANES
    if emit_state:
        dk_f_ref[...] = p_ref[:, n_dqk // 2:n_dqk]
    dv = p_ref[:, col:col + n_dv]
    qkv_ref[:, col:col + n_dv] = dv.astype(BF16)
    if emit_state:
        dv_f_ref[...] = dv
    col += n_dv
    for _ in range(n_gq // LANES):
        x = _rms(p_ref[:, col:col + LANES], gqn_ref[...])
        if rope:
            x = _rope(x, c128, lo128, hi128, HEAD_DIM // 4)
        qkv_ref[:, col:col + LANES] = x.astype(BF16)
        col += LANES
    for h in range(n_gk // LANES):
        x = _rms(p_ref[:, col:col + LANES], gkn_ref[...])
        if emit_state:
            gk_f_ref[:, h * LANES:(h + 1) * LANES] = x
        if rope:
            x = _rope(x, c128, lo128, hi128, HEAD_DIM // 4)
        qkv_ref[:, col:col + LANES] = x.astype(BF16)
        col += LANES
    gv = p_ref[:, col:col + n_gk]
    qkv_ref[:, col:col + n_gk] = gv.astype(BF16)
    if emit_state:
        gv_f_ref[...] = gv
    col += n_gk
    cq_ref[...] = _rms(p_ref[:, col:col + q_lora], mqn_ref[...]).astype(BF16)
    col += q_lora
    ckv = _rms(p_ref[:, col:col + kv_lora], mkvn_ref[...])
    ckv_ref[...] = ckv.astype(BF16)
    if emit_state:
        ckv_f_ref[...] = ckv
    kpe = pk_ref[...]
    if rope:
        kpe = _rope(kpe, c64, lo64, hi64, MLA_ROPE // 4)
    kpe_ref[...] = kpe.astype(BF16)


def _post(p, pk, l, dims, gqn, gkn, mqn, mkvn, tables, t_len, emit_state):
    m, n = p.shape
    n_dqk, n_dv, n_gq, n_gk, q_lora, kv_lora = dims
    n_qkv = n_dqk + n_dv + n_gq + 2 * n_gk
    rope = tables is not None
    tr = _row_tile(t_len if rope else m, 256)
    row_spec = lambda width: pl.BlockSpec((tr, width), lambda i: (i, 0))
    gain_spec = lambda width: pl.BlockSpec((None, 1, width), lambda i: (l, 0, 0))
    in_specs = [row_spec(n), row_spec(LANES), gain_spec(HEAD_DIM), gain_spec(HEAD_DIM), gain_spec(q_lora),
                gain_spec(kv_lora)]
    args = [p, pk, gqn, gkn, mqn, mkvn]
    if rope:
        nt = t_len // tr
        in_specs += [pl.BlockSpec((tr, LANES), lambda i: (i % nt, 0))] * 6
        args += list(tables)
    out_widths = [(n_qkv, BF16), (q_lora, BF16), (kv_lora, BF16), (LANES, BF16)]
    if emit_state:
        out_widths += [(n_dqk // 2, F32), (n_dv, F32), (n_gk, F32), (n_gk, F32), (kv_lora, F32)]
    kern = functools.partial(_post_kernel, dims=dims, rope=rope, emit_state=emit_state)
    return pl.pallas_call(
        kern,
        grid=(m // tr,),
        in_specs=in_specs,
        out_specs=[row_spec(width) for width, _ in out_widths],
        out_shape=[jax.ShapeDtypeStruct((m, width), dt) for width, dt in out_widths],
        compiler_params=_params(("arbitrary",)),
        name="mixer_post",
    )(*args)


def _attend(q_parts, segments, c):
    s = [_sum([_dot_nt(q, k) for q, k in zip(q_parts, ks)]) for ks, _ in segments]
    m = functools.reduce(jnp.maximum, [jnp.max(x, axis=-1, keepdims=True) for x in s])
    e = [jnp.exp2((x - m) * c) for x in s]
    den = _sum([jnp.sum(x, axis=-1, keepdims=True) for x in e])
    o = _sum([jnp.dot(x.astype(BF16), v, preferred_element_type=F32) for x, (_, v) in zip(e, segments)])
    return o / den


def _diff_attn_kernel(*refs, heads, has_cache, lam_init):
    q_ref, k_ref, v_ref = refs[:3]
    pos = 3
    if has_cache:
        kc_ref, vc_ref = refs[pos:pos + 2]
        pos += 2
    lq1, lk1, lq2, lk2, sub_ref, o_ref = refs[pos:pos + 6]
    lam = (jnp.exp(jnp.sum(lq1[...] * lk1[...], axis=-1, keepdims=True))
           - jnp.exp(jnp.sum(lq2[...] * lk2[...], axis=-1, keepdims=True)) + lam_init)
    c = DIFF_QK ** -0.5 * LOG2_E
    lane = lax.broadcasted_iota(jnp.int32, (q_ref.shape[0], LANES), 1)
    for h in range(heads):
        cols = slice(h * LANES, (h + 1) * LANES)
        q = q_ref[:, cols]
        zero = jnp.zeros_like(q)
        segments = [([k_ref[:, cols]], v_ref[:, cols])]
        if has_cache:
            segments.insert(0, ([kc_ref[:, cols].astype(BF16)], vc_ref[:, cols].astype(BF16)))
        o0 = _attend([jnp.where(lane < DIFF_QK, q, zero)], segments, c)
        o1 = _attend([jnp.where(lane >= DIFF_QK, q, zero)], segments, c)
        o = o0 - lam * o1
        o_ref[:, cols] = (_rms(o, sub_ref[...]) * (1.0 - lam_init)).astype(o_ref.dtype)


def _gqa_attn_kernel(*refs, kv_heads, group, has_cache):
    q_ref, k_ref, v_ref = refs[:3]
    pos = 3
    if has_cache:
        kc_ref, vc_ref = refs[pos:pos + 2]
        pos += 2
    o_ref = refs[pos]
    c = HEAD_DIM ** -0.5 * LOG2_E
    for kvh in range(kv_heads):
        cols = slice(kvh * LANES, (kvh + 1) * LANES)
        segments = [([k_ref[:, cols]], v_ref[:, cols])]
        if has_cache:
            segments.insert(0, ([kc_ref[:, cols].astype(BF16)], vc_ref[:, cols].astype(BF16)))
        for g in range(group):
            qcols = slice((kvh * group + g) * LANES, (kvh * group + g + 1) * LANES)
            o_ref[:, qcols] = _attend([q_ref[:, qcols]], segments, c).astype(o_ref.dtype)


def _mla_attn_kernel(*refs, heads, has_cache, rope):
    mq_ref, kv_ref, kp_ref = refs[:3]
    pos = 3
    if has_cache:
        kvc_ref, kpc_ref = refs[pos:pos + 2]
        pos += 2
    if rope:
        c64, lo64, hi64 = (r[...] for r in refs[pos:pos + 3])
        pos += 3
    o_ref = refs[pos]
    c = (MLA_NOPE + MLA_ROPE) ** -0.5 * LOG2_E
    kp = kp_ref[...]
    if has_cache:
        kpc = kpc_ref[...].astype(BF16)
    for h in range(heads):
        cols = slice(h * LANES, (h + 1) * LANES)
        kcols = slice(2 * h * LANES, (2 * h + 1) * LANES)
        vcols = slice((2 * h + 1) * LANES, (2 * h + 2) * LANES)
        qn = mq_ref[:, cols].astype(BF16)
        qp = mq_ref[:, (heads + h) * LANES:(heads + h + 1) * LANES]
        if rope:
            qp = _rope(qp, c64, lo64, hi64, MLA_ROPE // 4)
        q = jnp.concatenate([qn, qp.astype(BF16)], axis=-1)
        segments = [([jnp.concatenate([kv_ref[:, kcols], kp], axis=-1)], kv_ref[:, vcols])]
        if has_cache:
            segments.insert(0, ([jnp.concatenate([kvc_ref[:, kcols], kpc], axis=-1)], kvc_ref[:, vcols]))
        o_ref[:, cols] = _attend([q], segments, c).astype(o_ref.dtype)


def _q_tile(t):
    return 256 if t % 256 == 0 else t


def _attn_call(kern, name, batch, t, out_width, q_specs, kv_specs, cache_specs, extra_specs, args):
    tq = _q_tile(t)
    nq = t // tq
    in_specs = ([pl.BlockSpec((tq, w), lambda b, i, c=c: (b * nq + i, c)) for w, c in q_specs]
                + [pl.BlockSpec((t, w), lambda b, i, c=c: (b, c)) for w, c in kv_specs]
                + cache_specs + extra_specs)
    return pl.pallas_call(
        kern,
        grid=(batch, nq),
        in_specs=in_specs,
        out_specs=pl.BlockSpec((tq, out_width), lambda b, i: (b * nq + i, 0)),
        out_shape=jax.ShapeDtypeStruct((batch * t, out_width), BF16),
        compiler_params=_params(("arbitrary", "arbitrary")),
        name=name,
    )(*args)


def _cache_spec(l, s_cache, width, col):
    return pl.BlockSpec((None, None, s_cache, width), lambda b, i: (b, l, 0, col))


def _diff_attention(qkv, l, batch, t, heads, cache_k4, cache_v4, lparams, subln):
    has_cache = cache_k4 is not None
    w = heads * LANES
    cache_specs, args = [], [qkv, qkv, qkv]
    if has_cache:
        cache_specs = [_cache_spec(l, cache_k4.shape[2], w, 0)] * 2
        args += [cache_k4, cache_v4]
    extra = [pl.BlockSpec((None, 1, DIFF_QK), lambda b, i: (l, 0, 0))] * 4
    extra += [pl.BlockSpec((None, 1, DIFF_V), lambda b, i: (l, 0, 0))]
    args += list(lparams) + [subln]
    lam_init = 0.8 - 0.6 * math.exp(-0.3 * l)
    kern = functools.partial(_diff_attn_kernel, heads=heads, has_cache=has_cache, lam_init=lam_init)
    return _attn_call(kern, "diff_attention", batch, t, w, [(w, 0)], [(w, 1), (w, 2)], cache_specs, extra, args)


def _gqa_attention(qkv, l, batch, t, col0, q_heads, kv_heads, cache_k4, cache_v4):
    has_cache = cache_k4 is not None
    qw, kw = q_heads * HEAD_DIM, kv_heads * HEAD_DIM
    assert col0 % qw == 0 and (col0 + qw) % kw == 0
    k_blk = (col0 + qw) // kw
    cache_specs, args = [], [qkv, qkv, qkv]
    if has_cache:
        cache_specs = [_cache_spec(l, cache_k4.shape[2], kw, 0)] * 2
        args += [cache_k4, cache_v4]
    kern = functools.partial(_gqa_attn_kernel, kv_heads=kv_heads, group=q_heads // kv_heads, has_cache=has_cache)
    return _attn_call(kern, "gqa_attention", batch, t, qw, [(qw, col0 // qw)], [(kw, k_blk), (kw, k_blk + 1)],
                      cache_specs, [], args)


def _mla_attention(mq, kv, kpe, l, batch, t, heads, kv_cache, cache_kpe4, tables64):
    has_cache = kv_cache is not None
    rope = tables64 is not None
    tq = _q_tile(t)
    cache_specs, extra, args = [], [], [mq, kv, kpe]
    if has_cache:
        s_cache = cache_kpe4.shape[2]
        cache_specs = [pl.BlockSpec((s_cache, kv.shape[1]), lambda b, i: (b, 0)), _cache_spec(l, s_cache, LANES, 0)]
        args += [kv_cache, cache_kpe4]
    if rope:
        extra = [pl.BlockSpec((tq, LANES), lambda b, i: (i, 0))] * 3
        args += list(tables64)
    kern = functools.partial(_mla_attn_kernel, heads=heads, has_cache=has_cache, rope=rope)
    return _attn_call(kern, "mla_attention", batch, t, heads * MLA_V, [(mq.shape[1], 0)],
                      [(kv.shape[1], 0), (LANES, 0)], cache_specs, extra, args)


def _rope_tables(t_len, width):
    half = width // 2
    quarter = half // 2
    t = jnp.arange(t_len, dtype=jnp.int32)
    rows = (t // GRID_W).astype(F32)
    cols = (t % GRID_W).astype(F32)
    lane = jnp.arange(LANES, dtype=jnp.int32) % width
    inv = ROPE_THETA ** (-(jnp.arange(0, half, 2, dtype=F32) / half))
    inv_lane = inv[(lane % half) % quarter]
    pos = jnp.where((lane < half)[None, :], rows[:, None], cols[:, None])
    ang = pos * inv_lane[None, :]
    cos, sin = jnp.cos(ang), jnp.sin(ang)
    first = ((lane % half) < quarter)[None, :]
    zero = jnp.zeros_like(sin)
    return cos, jnp.where(first, -sin, zero), jnp.where(first, zero, sin)


def _layer(x, l, w, mod4, grp, batch, t, caches, tables, emit_state):
    ffn = lambda h, w_in, w_out, norm, c0, name: _matmul_resid(
        [_matmul_swiglu(_normmod(h, norm, l, mod4, grp, c0, c0 + 1), w_in, l)],
        w_out, l, h, mod4, grp, c0 + 2, 0.5, name)
    x = ffn(x, w["ffn1_w_in"], w["ffn1_w_out"], w["norm_ffn1"], 0, "ffn1_out")

    u = _normmod(x, w["norm_mix"], l, mod4, grp, 3, 4)
    dims = w["dims"]
    n_dqk, n_dv, n_gq, n_gk, q_lora, kv_lora = dims
    n_main = n_dqk + n_dv + n_gq + 2 * n_gk + q_lora + kv_lora
    p = _matmul(u, w["w_in"], l, F32, "mixer_in", n=n_main, tn_target=512)
    pk = _matmul_tail(u, w["w_in"], l, n_main, "mixer_in_kpe")
    post = _post(p, pk, l, dims, w["gqa_q_norm"], w["gqa_k_norm"], w["mla_q_norm"], w["mla_kv_norm"],
                 tables[0] + tables[1] if tables is not None else None, t, emit_state)
    qkv, cqn, ckvn, kpe = post[:4]

    diff_heads = n_dv // DIFF_V
    q_heads, kv_heads = n_gq // HEAD_DIM, n_gk // HEAD_DIM
    mla_heads = w["mla_w_ukv"].shape[-1] // (MLA_NOPE + MLA_V)
    has_cache = caches is not None
    o_diff = _diff_attention(qkv, l, batch, t, diff_heads,
                             caches["diff_k"] if has_cache else None, caches["diff_v"] if has_cache else None,
                             w["diff_l"], w["diff_subln"])
    o_gqa = _gqa_attention(qkv, l, batch, t, n_dqk + n_dv, q_heads, kv_heads,
                           caches["gqa_k"] if has_cache else None, caches["gqa_v"] if has_cache else None)
    mq = _matmul(cqn, w["mla_w_uq"], l, F32, "mla_q_up")
    kv = _matmul(ckvn, w["mla_w_ukv"], l, BF16, "mla_kv_up")
    kv_cache = _matmul_cache(caches["mla_ckv"], l, w["mla_w_ukv"], l, BF16, "mla_kv_up_cache") if has_cache else None
    o_mla = _mla_attention(mq, kv, kpe, l, batch, t, mla_heads, kv_cache,
                           caches["mla_kpe"] if has_cache else None, tables[1] if tables is not None else None)
    x = _matmul_resid([o_diff, o_gqa, o_mla], w["w_out"], l, x, mod4, grp, 5, 1.0, "mixer_out", tn_target=512)

    x = ffn(x, w["ffn2_w_in"], w["ffn2_w_out"], w["norm_ffn2"], 6, "ffn2_out")
    state = list(post[4:]) + [pk[:, :MLA_ROPE]] if emit_state else None
    return x, state


def kernel(x_prompt, x_sample, cache_diff_k, cache_diff_v, cache_gqa_k, cache_gqa_v, cache_mla_ckv, cache_mla_kpe, c, c_ctx, w_mod, b_mod, norm_ffn1, norm_mix, norm_ffn2, ffn1_w_in, ffn1_w_out, ffn2_w_in, ffn2_w_out, w_in, w_out, diff_lq1, diff_lk1, diff_lq2, diff_lk2, diff_subln, gqa_q_norm, gqa_k_norm, mla_q_norm, mla_kv_norm, mla_w_uq, mla_w_ukv, final_norm):
    batch, seq, d = x_prompt.shape
    dec_batch, dec_seq, _ = x_sample.shape
    depth = w_mod.shape[0]
    past = cache_diff_k.shape[2]
    diff_heads = cache_diff_k.shape[3]
    kv_heads = cache_gqa_k.shape[3]
    q_lora, kv_lora = mla_q_norm.shape[-1], mla_kv_norm.shape[-1]
    mla_heads = mla_w_ukv.shape[-1] // (MLA_NOPE + MLA_V)
    n_dqk = 2 * diff_heads * 2 * DIFF_QK
    n_dv = diff_heads * DIFF_V
    n_gk = kv_heads * HEAD_DIM
    n_gq = w_in.shape[-1] - (n_dqk + n_dv + 2 * n_gk + q_lora + kv_lora + MLA_ROPE)
    dims = (n_dqk, n_dv, n_gq, n_gk, q_lora, kv_lora)
    assert dec_batch + 1 <= MOD_ROWS

    cond = jnp.zeros((MOD_ROWS, d), F32).at[0].set(c_ctx).at[1:1 + dec_batch].set(c)
    mod4 = _modulation(cond, w_mod, b_mod).reshape(depth, MOD_ROWS, 1, N_MOD * d)

    uq = mla_w_uq.reshape(depth, q_lora, mla_heads, MLA_NOPE + MLA_ROPE)
    uq_pe = jnp.pad(uq[..., MLA_NOPE:], ((0, 0), (0, 0), (0, 0), (0, LANES - MLA_ROPE)))
    uq = jnp.concatenate([uq[..., :MLA_NOPE].reshape(depth, q_lora, -1), uq_pe.reshape(depth, q_lora, -1)], axis=-1)
    cache_kpe = jnp.pad(cache_mla_kpe, ((0, 0), (0, 0), (0, 0), (0, LANES - MLA_ROPE)))

    r3 = lambda a: a.reshape(depth, 1, a.shape[-1])
    weights = {
        "ffn1_w_in": ffn1_w_in, "ffn1_w_out": ffn1_w_out, "ffn2_w_in": ffn2_w_in, "ffn2_w_out": ffn2_w_out,
        "w_in": w_in, "w_out": w_out, "mla_w_uq": uq, "mla_w_ukv": mla_w_ukv,
        "norm_ffn1": r3(norm_ffn1), "norm_mix": r3(norm_mix), "norm_ffn2": r3(norm_ffn2),
        "gqa_q_norm": r3(gqa_q_norm), "gqa_k_norm": r3(gqa_k_norm),
        "mla_q_norm": r3(mla_q_norm), "mla_kv_norm": r3(mla_kv_norm),
        "diff_l": (r3(diff_lq1), r3(diff_lk1), r3(diff_lq2), r3(diff_lk2)), "diff_subln": r3(diff_subln),
        "dims": dims,
    }
    caches = {
        "diff_k": cache_diff_k.reshape(dec_batch, depth, past, -1),
        "diff_v": cache_diff_v.reshape(dec_batch, depth, past, -1),
        "gqa_k": cache_gqa_k.reshape(dec_batch, depth, past, -1),
        "gqa_v": cache_gqa_v.reshape(dec_batch, depth, past, -1),
        "mla_ckv": cache_mla_ckv, "mla_kpe": cache_kpe,
    }
    tables = (_rope_tables(dec_seq, HEAD_DIM), _rope_tables(dec_seq, DIFF_QK))

    h = x_prompt.reshape(batch * seq, d)
    states = []
    for l in range(depth):
        h, st = _layer(h, l, weights, mod4, (0, batch * seq), batch, seq, None, None, True)
        states.append(st)
    y_prompt = _final_norm(h, final_norm).reshape(batch, seq, d)

    h = x_sample.reshape(dec_batch * dec_seq, d)
    for l in range(depth):
        h, _ = _layer(h, l, weights, mod4, (1, dec_seq), dec_batch, dec_seq, caches, tables, False)
    y_sample = _final_norm(h, final_norm).reshape(dec_batch, dec_seq, d)

    stack = lambda i, shape: jnp.stack([s[i].reshape((batch, seq) + shape) for s in states], axis=1)
    return (y_prompt, y_sample,
            stack(0, (diff_heads, 2, DIFF_QK)), stack(1, (diff_heads, DIFF_V)),
            stack(2, (kv_heads, HEAD_DIM)), stack(3, (kv_heads, HEAD_DIM)),
            stack(4, (kv_lora,)), stack(5, (MLA_ROPE,)))
```

```python
import functools
import math
import operator

import jax
import jax.numpy as jnp
from jax import lax
from jax.experimental import pallas as pl
from jax.experimental.pallas import tpu as pltpu

F32 = jnp.float32
BF16 = jnp.bfloat16

NORM_EPS = 1e-6
ROPE_THETA = 10000.0
GRID_W = 64
N_MOD = 9
DIFF_QK = 64
DIFF_V = 128
HEAD_DIM = 128
MLA_NOPE = 128
MLA_ROPE = 64
MLA_V = 128
LANES = 128
MOD_ROWS = 16
VMEM_LIMIT_BYTES = 56 * 1024 * 1024
W_TILE_BYTES_MAX = 8 * 1024 * 1024
LOG2_E = 1.4426950408889634


def _params(sem):
    return pltpu.CompilerParams(dimension_semantics=sem, vmem_limit_bytes=VMEM_LIMIT_BYTES)


def _silu(x):
    return x / (1.0 + jnp.exp(-x))


def _rms(x, gain):
    y = x * lax.rsqrt(jnp.mean(x * x, axis=-1, keepdims=True) + NORM_EPS)
    return y * gain


def _rope(x, cos, sin_lo, sin_hi, shift):
    return (x * cos + pltpu.roll(x, LANES - shift, 1) * sin_lo + pltpu.roll(x, shift, 1) * sin_hi)


_NT = (((1,), (1,)), ((), ()))


def _dot_nt(a, b):
    return lax.dot_general(a, b, _NT, preferred_element_type=F32)


def _sum(xs):
    return functools.reduce(operator.add, xs)


def _mod_kernel(c_ref, w_ref, b_ref, o_ref):
    a = _silu(c_ref[...]).astype(BF16)
    o_ref[...] = jnp.dot(a, w_ref[...].astype(BF16), preferred_element_type=F32) + b_ref[...]


def _modulation(cond, w_mod, b_mod):
    depth, d, n = w_mod.shape
    tn = 512 if n % 512 == 0 else n
    return pl.pallas_call(
        _mod_kernel,
        grid=(depth, n // tn),
        in_specs=[
            pl.BlockSpec((MOD_ROWS, d), lambda l, j: (0, 0)),
            pl.BlockSpec((None, d, tn), lambda l, j: (l, 0, j)),
            pl.BlockSpec((None, 1, tn), lambda l, j: (l, 0, j)),
        ],
        out_specs=pl.BlockSpec((None, MOD_ROWS, tn), lambda l, j: (l, 0, j)),
        out_shape=jax.ShapeDtypeStruct((depth, MOD_ROWS, n), F32),
        compiler_params=_params(("arbitrary", "arbitrary")),
        name="modulation",
    )(cond, w_mod, b_mod.reshape(depth, 1, n))


NORM_ROWS = 16


def _norm_rows(x_ref, o_ref, gain, shift):
    def body(r, carry):
        rows = pl.ds(pl.multiple_of(r * NORM_ROWS, NORM_ROWS), NORM_ROWS)
        x = x_ref[rows, :]
        y = x * lax.rsqrt(jnp.mean(x * x, axis=-1, keepdims=True) + NORM_EPS) * gain
        o_ref[rows, :] = (y if shift is None else y + shift).astype(o_ref.dtype)
        return carry

    lax.fori_loop(0, x_ref.shape[0] // NORM_ROWS, body, 0, unroll=8)


def _normmod_kernel(x_ref, g_ref, sh_ref, sc_ref, o_ref):
    _norm_rows(x_ref, o_ref, g_ref[...] * (1.0 + sc_ref[...]), sh_ref[...])


def _norm_kernel(x_ref, g_ref, o_ref):
    _norm_rows(x_ref, o_ref, g_ref[...], None)


def _row_tile(m, target):
    t = min(m, target)
    assert m % t == 0
    return t


def _normmod(x, gain3, l, mod4, grp, chunk_shift, chunk_scale):
    m, d = x.shape
    g_base, g_rows = grp
    tr = _row_tile(g_rows, 512)
    return pl.pallas_call(
        _normmod_kernel,
        grid=(m // tr,),
        in_specs=[
            pl.BlockSpec((tr, d), lambda i: (i, 0)),
            pl.BlockSpec((None, 1, d), lambda i: (l, 0, 0)),
            pl.BlockSpec((None, None, 1, d), lambda i: (l, g_base + (i * tr) // g_rows, 0, chunk_shift)),
            pl.BlockSpec((None, None, 1, d), lambda i: (l, g_base + (i * tr) // g_rows, 0, chunk_scale)),
        ],
        out_specs=pl.BlockSpec((tr, d), lambda i: (i, 0)),
        out_shape=jax.ShapeDtypeStruct((m, d), BF16),
        compiler_params=_params(("arbitrary",)),
        name="normmod",
    )(x, gain3, mod4, mod4)


def _final_norm(x, gain):
    m, d = x.shape
    tr = _row_tile(m, 512)
    return pl.pallas_call(
        _norm_kernel,
        grid=(m // tr,),
        in_specs=[pl.BlockSpec((tr, d), lambda i: (i, 0)), pl.BlockSpec((1, d), lambda i: (0, 0))],
        out_specs=pl.BlockSpec((tr, d), lambda i: (i, 0)),
        out_shape=jax.ShapeDtypeStruct((m, d), F32),
        compiler_params=_params(("arbitrary",)),
        name="final_norm",
    )(x, gain.reshape(1, d))


def _mm_kernel(a_ref, w_ref, o_ref):
    a = a_ref[...].astype(BF16)
    o_ref[...] = jnp.dot(a, w_ref[...].astype(BF16), preferred_element_type=F32).astype(o_ref.dtype)


def _mm_t_kernel(a_ref, wt_ref, o_ref):
    o_ref[...] = _dot_nt(a_ref[...], wt_ref[...].astype(BF16)).astype(o_ref.dtype)


def _mm_t_tail_kernel(a_ref, wt_ref, o_ref, *, valid):
    wt = wt_ref[...]
    row = lax.broadcasted_iota(jnp.int32, wt.shape, 0)
    wt = jnp.where(row < valid, wt, 0.0).astype(BF16)
    o_ref[...] = _dot_nt(a_ref[...], wt)


def _mm_swiglu_kernel(a_ref, wg_ref, wu_ref, o_ref):
    a = a_ref[...]
    g = jnp.dot(a, wg_ref[...].astype(BF16), preferred_element_type=F32)
    u = jnp.dot(a, wu_ref[...].astype(BF16), preferred_element_type=F32)
    o_ref[...] = (_silu(g) * u).astype(o_ref.dtype)


def _mm_resid_kernel(*refs, widths, coef, nk, tk):
    n_a = len(widths)
    a_refs = refs[:n_a]
    w_ref, x_ref, gate_ref, o_ref, acc_ref = refs[n_a:]
    if nk == 1:
        row = 0
        parts = []
        for a_ref, width in zip(a_refs, widths):
            parts.append(jnp.dot(a_ref[...], w_ref[row:row + width, :].astype(BF16), preferred_element_type=F32))
            row += width
        o_ref[...] = x_ref[...] + (coef * gate_ref[...]) * _sum(parts)
        return
    k = pl.program_id(2)
    a_ref, = a_refs
    for kk in range(nk):
        @pl.when(k == kk)
        def _(kk=kk):
            part = jnp.dot(a_ref[:, kk * tk:(kk + 1) * tk], w_ref[...].astype(BF16), preferred_element_type=F32)
            if kk > 0:
                part = acc_ref[...] + part
            if kk == nk - 1:
                o_ref[...] = x_ref[...] + (coef * gate_ref[...]) * part
            else:
                acc_ref[...] = part


def _col_tile(n, target=256):
    return target if n % target == 0 else n


def _matmul_t_tail(a, wt, l, row0, name):
    m, k = a.shape
    valid = wt.shape[1] - row0
    assert row0 % LANES == 0 and 0 < valid <= LANES
    tm = _row_tile(m, 1024)
    return pl.pallas_call(
        functools.partial(_mm_t_tail_kernel, valid=valid),
        grid=(m // tm,),
        in_specs=[
            pl.BlockSpec((tm, k), lambda i: (i, 0)),
            pl.BlockSpec((None, LANES, k), lambda i: (l, row0 // LANES, 0)),
        ],
        out_specs=pl.BlockSpec((tm, LANES), lambda i: (i, 0)),
        out_shape=jax.ShapeDtypeStruct((m, LANES), F32),
        compiler_params=_params(("arbitrary",)),
        name=name,
    )(a, wt)


def _matmul_t(a, wt, l, out_dtype, name, n, tn_target=256):
    m, k = a.shape
    tm = _row_tile(m, 1024)
    tn = _col_tile(n, tn_target)
    return pl.pallas_call(
        _mm_t_kernel,
        grid=(m // tm, n // tn),
        in_specs=[
            pl.BlockSpec((tm, k), lambda i, j: (i, 0)),
            pl.BlockSpec((None, tn, k), lambda i, j: (l, j, 0)),
        ],
        out_specs=pl.BlockSpec((tm, tn), lambda i, j: (i, j)),
        out_shape=jax.ShapeDtypeStruct((m, n), out_dtype),
        compiler_params=_params(("arbitrary", "arbitrary")),
        name=name,
    )(a, wt)


def _matmul(a, w, l, out_dtype, name, n=None, tn_target=256):
    m, k = a.shape
    n = w.shape[-1] if n is None else n
    tm = _row_tile(m, 1024)
    tn = _col_tile(n, tn_target)
    return pl.pallas_call(
        _mm_kernel,
        grid=(m // tm, n // tn),
        in_specs=[
            pl.BlockSpec((tm, k), lambda i, j: (i, 0)),
            pl.BlockSpec((None, k, tn), lambda i, j: (l, 0, j)),
        ],
        out_specs=pl.BlockSpec((tm, tn), lambda i, j: (i, j)),
        out_shape=jax.ShapeDtypeStruct((m, n), out_dtype),
        compiler_params=_params(("arbitrary", "arbitrary")),
        name=name,
    )(a, w)


def _matmul_cache(a4, l, w, lw, out_dtype, name):
    b, _, s, k = a4.shape
    n = w.shape[-1]
    tn = n
    return pl.pallas_call(
        _mm_kernel,
        grid=(b, n // tn),
        in_specs=[
            pl.BlockSpec((None, None, s, k), lambda i, j: (i, l, 0, 0)),
            pl.BlockSpec((None, k, tn), lambda i, j: (lw, 0, j)),
        ],
        out_specs=pl.BlockSpec((s, tn), lambda i, j: (i, j)),
        out_shape=jax.ShapeDtypeStruct((b * s, n), out_dtype),
        compiler_params=_params(("arbitrary", "arbitrary")),
        name=name,
    )(a4, w)


def _matmul_swiglu(a, w, l):
    m, k = a.shape
    f = w.shape[-1] // 2
    tm = _row_tile(m, 1024)
    tn = _col_tile(f)
    nj = f // tn
    return pl.pallas_call(
        _mm_swiglu_kernel,
        grid=(m // tm, nj),
        in_specs=[
            pl.BlockSpec((tm, k), lambda i, j: (i, 0)),
            pl.BlockSpec((None, k, tn), lambda i, j: (l, 0, j)),
            pl.BlockSpec((None, k, tn), lambda i, j: (l, 0, j + nj)),
        ],
        out_specs=pl.BlockSpec((tm, tn), lambda i, j: (i, j)),
        out_shape=jax.ShapeDtypeStruct((m, f), BF16),
        compiler_params=_params(("arbitrary", "arbitrary")),
        name="ffn_in",
    )(a, w, w)


def _matmul_resid(a_parts, w, l, x, mod4, grp, chunk, coef, name, tn_target=256):
    m = a_parts[0].shape[0]
    widths = tuple(a.shape[1] for a in a_parts)
    k = sum(widths)
    n = w.shape[-1]
    g_base, g_rows = grp
    tm = _row_tile(g_rows, 1024)
    tn = _col_tile(n, tn_target)
    nk = 2 if (len(a_parts) == 1 and k * tn * 4 > W_TILE_BYTES_MAX and (k // 2) % LANES == 0) else 1
    tk = k // nk
    per_chunk = n // tn
    kern = functools.partial(_mm_resid_kernel, widths=widths, coef=coef, nk=nk, tk=tk)
    a_mode = {"pipeline_mode": pl.Buffered(1)} if nk > 1 else {}
    return pl.pallas_call(
        kern,
        grid=(m // tm, n // tn, nk),
        in_specs=[pl.BlockSpec((tm, width), lambda i, j, kk: (i, 0), **a_mode) for width in widths] + [
            pl.BlockSpec((None, tk, tn), lambda i, j, kk: (l, kk, j)),
            pl.BlockSpec((tm, tn), lambda i, j, kk: (i, j)),
            pl.BlockSpec((None, None, 1, tn), lambda i, j, kk: (l, g_base + (i * tm) // g_rows, 0, chunk * per_chunk + j)),
        ],
        out_specs=pl.BlockSpec((tm, tn), lambda i, j, kk: (i, j)),
        out_shape=jax.ShapeDtypeStruct((m, n), F32),
        scratch_shapes=[pltpu.VMEM((tm, tn), F32)],
        compiler_params=_params(("arbitrary", "arbitrary", "arbitrary")),
        name=name,
    )(*a_parts, w, x, mod4)


def _post_kernel(*refs, dims, rope, emit_state, n_alias):
    n_dqk, n_dv, n_gq, n_gk, q_lora, kv_lora = dims
    p_ref, pk_ref, gqn_ref, gkn_ref, mqn_ref, mkvn_ref = refs[:6]
    pos = 6
    if rope:
        c128, lo128, hi128, c64, lo64, hi64 = (r[...] for r in refs[pos:pos + 6])
        pos += 6
    pos += n_alias
    qkv_ref, cq_ref, ckv_ref, kpe_ref = refs[pos:pos + 4]
    pos += 4
    if emit_state:
        dk_f_ref, dv_f_ref, gk_f_ref, gv_f_ref, ckv_f_ref = refs[pos:pos + 5]

    col = 0
    for _ in range(n_dqk // LANES):
        x = p_ref[:, col:col + LANES]
        if rope:
            x = _rope(x, c64, lo64, hi64, DIFF_QK // 4)
        qkv_ref[:, col:col + LANES] = x.astype(BF16)
        col += LANES
    if emit_state:
        dk_f_ref[...] = p_ref[:, n_dqk // 2:n_dqk]
    dv = p_ref[:, col:col + n_dv]
    qkv_ref[:, col:col + n_dv] = dv.astype(BF16)
    if emit_state:
        dv_f_ref[...] = dv
    col += n_dv
    for _ in range(n_gq // LANES):
        x = _rms(p_ref[:, col:col + LANES], gqn_ref[...])
        if rope:
            x = _rope(x, c128, lo128, hi128, HEAD_DIM // 4)
        qkv_ref[:, col:col + LANES] = x.astype(BF16)
        col += LANES
    for h in range(n_gk // LANES):
        x = _rms(p_ref[:, col:col + LANES], gkn_ref[...])
        if emit_state:
            gk_f_ref[:, h * LANES:(h + 1) * LANES] = x
        if rope:
            x = _rope(x, c128, lo128, hi128, HEAD_DIM // 4)
        qkv_ref[:, col:col + LANES] = x.astype(BF16)
        col += LANES
    gv = p_ref[:, col:col + n_gk]
    qkv_ref[:, col:col + n_gk] = gv.astype(BF16)
    if emit_state:
        gv_f_ref[...] = gv
    col += n_gk
    cq_ref[...] = _rms(p_ref[:, col:col + q_lora], mqn_ref[...]).astype(BF16)
    col += q_lora
    ckv = _rms(p_ref[:, col:col + kv_lora], mkvn_ref[...])
    ckv_ref[...] = ckv.astype(BF16)
    if emit_state:
        ckv_f_ref[...] = ckv
    kpe = pk_ref[...]
    if rope:
        kpe = _rope(kpe, c64, lo64, hi64, MLA_ROPE // 4)
    kpe_ref[...] = kpe.astype(BF16)


def _post(p, pk, l, depth, dims, gqn, gkn, mqn, mkvn, tables, t_len, emit_state, state_in):
    m, n = p.shape
    n_dqk, n_dv, n_gq, n_gk, q_lora, kv_lora = dims
    n_qkv = n_dqk + n_dv + n_gq + 2 * n_gk
    rope = tables is not None
    tr = _row_tile(t_len, 256)
    nt = t_len // tr
    row_spec = lambda width: pl.BlockSpec((tr, width), lambda i: (i, 0))
    gain_spec = lambda width: pl.BlockSpec((None, 1, width), lambda i: (l, 0, 0))
    in_specs = [row_spec(n), row_spec(LANES), gain_spec(HEAD_DIM), gain_spec(HEAD_DIM), gain_spec(q_lora),
                gain_spec(kv_lora)]
    args = [p, pk, gqn, gkn, mqn, mkvn]
    if rope:
        in_specs += [pl.BlockSpec((tr, LANES), lambda i: (i % nt, 0))] * 6
        args += list(tables)
    out_widths = [(n_qkv, BF16), (q_lora, BF16), (kv_lora, BF16), (LANES, BF16)]
    out_specs = [row_spec(width) for width, _ in out_widths]
    out_shape = [jax.ShapeDtypeStruct((m, width), dt) for width, dt in out_widths]
    aliases = {}
    n_alias = 0
    if emit_state:
        state_widths = [n_dqk // 2, n_dv, n_gk, n_gk, kv_lora]
        if state_in is not None:
            n_alias = len(state_in)
            aliases = {len(args) + s: len(out_shape) + s for s in range(n_alias)}
            in_specs += [pl.BlockSpec(memory_space=pl.ANY)] * n_alias
            args += list(state_in)
        out_specs += [pl.BlockSpec((None, None, tr, width), lambda i: (i // nt, l, i % nt, 0)) for width in state_widths]
        out_shape += [jax.ShapeDtypeStruct((m // t_len, depth, t_len, width), F32) for width in state_widths]
    kern = functools.partial(_post_kernel, dims=dims, rope=rope, emit_state=emit_state, n_alias=n_alias)
    return pl.pallas_call(
        kern,
        grid=(m // tr,),
        in_specs=in_specs,
        out_specs=out_specs,
        out_shape=out_shape,
        input_output_aliases=aliases,
        compiler_params=_params(("arbitrary",)),
        name="mixer_post",
    )(*args)


def _attend(q_parts, segments, c):
    s = [_sum([_dot_nt(q, k) for q, k in zip(q_parts, ks)]) for ks, _ in segments]
    m = functools.reduce(jnp.maximum, [jnp.max(x, axis=-1, keepdims=True) for x in s])
    e = [jnp.exp2((x - m) * c) for x in s]
    den = _sum([jnp.sum(x, axis=-1, keepdims=True) for x in e])
    o = _sum([jnp.dot(x.astype(BF16), v, preferred_element_type=F32) for x, (_, v) in zip(e, segments)])
    return o / den


def _diff_attn_kernel(*refs, heads, has_cache, lam_init):
    q_ref, k_ref, v_ref = refs[:3]
    pos = 3
    if has_cache:
        kc_ref, vc_ref = refs[pos:pos + 2]
        pos += 2
    lq1, lk1, lq2, lk2, sub_ref, o_ref = refs[pos:pos + 6]
    lam = (jnp.exp(jnp.sum(lq1[...] * lk1[...], axis=-1, keepdims=True))
           - jnp.exp(jnp.sum(lq2[...] * lk2[...], axis=-1, keepdims=True)) + lam_init)
    c = DIFF_QK ** -0.5 * LOG2_E
    lane = lax.broadcasted_iota(jnp.int32, (q_ref.shape[0], LANES), 1)
    for h in range(heads):
        cols = slice(h * LANES, (h + 1) * LANES)
        q = q_ref[:, cols]
        zero = jnp.zeros_like(q)
        segments = [([k_ref[:, cols]], v_ref[:, cols])]
        if has_cache:
            segments.insert(0, ([kc_ref[:, cols].astype(BF16)], vc_ref[:, cols].astype(BF16)))
        o0 = _attend([jnp.where(lane < DIFF_QK, q, zero)], segments, c)
        o1 = _attend([jnp.where(lane >= DIFF_QK, q, zero)], segments, c)
        o = o0 - lam * o1
        o_ref[:, cols] = (_rms(o, sub_ref[...]) * (1.0 - lam_init)).astype(o_ref.dtype)


def _gqa_attn_kernel(*refs, kv_heads, group, has_cache):
    q_ref, k_ref, v_ref = refs[:3]
    pos = 3
    if has_cache:
        kc_ref, vc_ref = refs[pos:pos + 2]
        pos += 2
    o_ref = refs[pos]
    c = HEAD_DIM ** -0.5 * LOG2_E
    for kvh in range(kv_heads):
        cols = slice(kvh * LANES, (kvh + 1) * LANES)
        segments = [([k_ref[:, cols]], v_ref[:, cols])]
        if has_cache:
            segments.insert(0, ([kc_ref[:, cols].astype(BF16)], vc_ref[:, cols].astype(BF16)))
        for g in range(group):
            qcols = slice((kvh * group + g) * LANES, (kvh * group + g + 1) * LANES)
            o_ref[:, qcols] = _attend([q_ref[:, qcols]], segments, c).astype(o_ref.dtype)


def _mla_attn_kernel(*refs, heads, has_cache, rope):
    mq_ref, kv_ref, kp_ref = refs[:3]
    pos = 3
    if has_cache:
        kvc_ref, kpc_ref = refs[pos:pos + 2]
        pos += 2
    if rope:
        c64, lo64, hi64 = (r[...] for r in refs[pos:pos + 3])
        pos += 3
    o_ref = refs[pos]
    c = (MLA_NOPE + MLA_ROPE) ** -0.5 * LOG2_E
    kp = kp_ref[...]
    if has_cache:
        kpc = kpc_ref[...].astype(BF16)
    for h in range(heads):
        cols = slice(h * LANES, (h + 1) * LANES)
        kcols = slice(2 * h * LANES, (2 * h + 1) * LANES)
        vcols = slice((2 * h + 1) * LANES, (2 * h + 2) * LANES)
        qn = mq_ref[:, cols].astype(BF16)
        qp = mq_ref[:, (heads + h) * LANES:(heads + h + 1) * LANES]
        if rope:
            qp = _rope(qp, c64, lo64, hi64, MLA_ROPE // 4)
        q = jnp.concatenate([qn, qp.astype(BF16)], axis=-1)
        segments = [([jnp.concatenate([kv_ref[:, kcols], kp], axis=-1)], kv_ref[:, vcols])]
        if has_cache:
            segments.insert(0, ([jnp.concatenate([kvc_ref[:, kcols], kpc], axis=-1)], kvc_ref[:, vcols]))
        o_ref[:, cols] = _attend([q], segments, c).astype(o_ref.dtype)


def _q_tile(t):
    return 256 if t % 256 == 0 else t


def _attn_call(kern, name, batch, t, out_width, q_specs, kv_specs, cache_specs, extra_specs, args):
    tq = _q_tile(t)
    nq = t // tq
    in_specs = ([pl.BlockSpec((tq, w), lambda b, i, c=c: (b * nq + i, c)) for w, c in q_specs]
                + [pl.BlockSpec((t, w), lambda b, i, c=c: (b, c)) for w, c in kv_specs]
                + cache_specs + extra_specs)
    return pl.pallas_call(
        kern,
        grid=(batch, nq),
        in_specs=in_specs,
        out_specs=pl.BlockSpec((tq, out_width), lambda b, i: (b * nq + i, 0)),
        out_shape=jax.ShapeDtypeStruct((batch * t, out_width), BF16),
        compiler_params=_params(("arbitrary", "arbitrary")),
        name=name,
    )(*args)


def _cache_spec(l, s_cache, width, col):
    return pl.BlockSpec((None, None, s_cache, width), lambda b, i: (b, l, 0, col))


def _diff_attention(qkv, l, batch, t, heads, cache_k4, cache_v4, lparams, subln):
    has_cache = cache_k4 is not None
    w = heads * LANES
    cache_specs, args = [], [qkv, qkv, qkv]
    if has_cache:
        cache_specs = [_cache_spec(l, cache_k4.shape[2], w, 0)] * 2
        args += [cache_k4, cache_v4]
    extra = [pl.BlockSpec((None, 1, DIFF_QK), lambda b, i: (l, 0, 0))] * 4
    extra += [pl.BlockSpec((None, 1, DIFF_V), lambda b, i: (l, 0, 0))]
    args += list(lparams) + [subln]
    lam_init = 0.8 - 0.6 * math.exp(-0.3 * l)
    kern = functools.partial(_diff_attn_kernel, heads=heads, has_cache=has_cache, lam_init=lam_init)
    return _attn_call(kern, "diff_attention", batch, t, w, [(w, 0)], [(w, 1), (w, 2)], cache_specs, extra, args)


def _gqa_attention(qkv, l, batch, t, col0, q_heads, kv_heads, cache_k4, cache_v4):
    has_cache = cache_k4 is not None
    qw, kw = q_heads * HEAD_DIM, kv_heads * HEAD_DIM
    assert col0 % qw == 0 and (col0 + qw) % kw == 0
    k_blk = (col0 + qw) // kw
    cache_specs, args = [], [qkv, qkv, qkv]
    if has_cache:
        cache_specs = [_cache_spec(l, cache_k4.shape[2], kw, 0)] * 2
        args += [cache_k4, cache_v4]
    kern = functools.partial(_gqa_attn_kernel, kv_heads=kv_heads, group=q_heads // kv_heads, has_cache=has_cache)
    return _attn_call(kern, "gqa_attention", batch, t, qw, [(qw, col0 // qw)], [(kw, k_blk), (kw, k_blk + 1)],
                      cache_specs, [], args)


def _mla_attention(mq, kv, kpe, l, batch, t, heads, kv_cache, cache_kpe4, tables64):
    has_cache = kv_cache is not None
    rope = tables64 is not None
    tq = _q_tile(t)
    cache_specs, extra, args = [], [], [mq, kv, kpe]
    if has_cache:
        s_cache = cache_kpe4.shape[2]
        cache_specs = [pl.BlockSpec((s_cache, kv.shape[1]), lambda b, i: (b, 0)), _cache_spec(l, s_cache, LANES, 0)]
        args += [kv_cache, cache_kpe4]
    if rope:
        extra = [pl.BlockSpec((tq, LANES), lambda b, i: (i, 0))] * 3
        args += list(tables64)
    kern = functools.partial(_mla_attn_kernel, heads=heads, has_cache=has_cache, rope=rope)
    return _attn_call(kern, "mla_attention", batch, t, heads * MLA_V, [(mq.shape[1], 0)],
                      [(kv.shape[1], 0), (LANES, 0)], cache_specs, extra, args)


def _rope_tables(t_len, width):
    half = width // 2
    quarter = half // 2
    t = jnp.arange(t_len, dtype=jnp.int32)
    rows = (t // GRID_W).astype(F32)
    cols = (t % GRID_W).astype(F32)
    lane = jnp.arange(LANES, dtype=jnp.int32) % width
    inv = ROPE_THETA ** (-(jnp.arange(0, half, 2, dtype=F32) / half))
    inv_lane = inv[(lane % half) % quarter]
    pos = jnp.where((lane < half)[None, :], rows[:, None], cols[:, None])
    ang = pos * inv_lane[None, :]
    cos, sin = jnp.cos(ang), jnp.sin(ang)
    first = ((lane % half) < quarter)[None, :]
    zero = jnp.zeros_like(sin)
    return cos, jnp.where(first, -sin, zero), jnp.where(first, zero, sin)


def _layer(x, l, w, mod4, grp, batch, t, caches, tables, emit_state, state_in=None):
    ffn = lambda h, w_in, w_out, norm, c0, name: _matmul_resid(
        [_matmul_swiglu(_normmod(h, norm, l, mod4, grp, c0, c0 + 1), w_in, l)],
        w_out, l, h, mod4, grp, c0 + 2, 0.5, name)
    x = ffn(x, w["ffn1_w_in"], w["ffn1_w_out"], w["norm_ffn1"], 0, "ffn1_out")

    u = _normmod(x, w["norm_mix"], l, mod4, grp, 3, 4)
    dims = w["dims"]
    n_dqk, n_dv, n_gq, n_gk, q_lora, kv_lora = dims
    n_main = n_dqk + n_dv + n_gq + 2 * n_gk + q_lora + kv_lora
    p = _matmul_t(u, w["w_in_t"], l, F32, "mixer_in", n_main, tn_target=512)
    pk = _matmul_t_tail(u, w["w_in_t"], l, n_main, "mixer_in_kpe")
    post = _post(p, pk, l, w["ffn1_w_in"].shape[0], dims, w["gqa_q_norm"], w["gqa_k_norm"], w["mla_q_norm"],
                 w["mla_kv_norm"], tables[0] + tables[1] if tables is not None else None, t, emit_state, state_in)
    qkv, cqn, ckvn, kpe = post[:4]

    diff_heads = n_dv // DIFF_V
    q_heads, kv_heads = n_gq // HEAD_DIM, n_gk // HEAD_DIM
    mla_heads = w["mla_w_ukv"].shape[-1] // (MLA_NOPE + MLA_V)
    has_cache = caches is not None
    o_diff = _diff_attention(qkv, l, batch, t, diff_heads,
                             caches["diff_k"] if has_cache else None, caches["diff_v"] if has_cache else None,
                             w["diff_l"], w["diff_subln"])
    o_gqa = _gqa_attention(qkv, l, batch, t, n_dqk + n_dv, q_heads, kv_heads,
                           caches["gqa_k"] if has_cache else None, caches["gqa_v"] if has_cache else None)
    mq = _matmul(cqn, w["mla_w_uq"], l, F32, "mla_q_up", tn_target=1024)
    kv = _matmul(ckvn, w["mla_w_ukv"], l, BF16, "mla_kv_up", tn_target=w["mla_w_ukv"].shape[-1])
    kv_cache = _matmul_cache(caches["mla_ckv"], l, w["mla_w_ukv"], l, BF16, "mla_kv_up_cache") if has_cache else None
    o_mla = _mla_attention(mq, kv, kpe, l, batch, t, mla_heads, kv_cache,
                           caches["mla_kpe"] if has_cache else None, tables[1] if tables is not None else None)
    x = _matmul_resid([o_diff, o_gqa, o_mla], w["w_out"], l, x, mod4, grp, 5, 1.0, "mixer_out", tn_target=512)

    x = ffn(x, w["ffn2_w_in"], w["ffn2_w_out"], w["norm_ffn2"], 6, "ffn2_out")
    state = (list(post[4:]), pk[:, :MLA_ROPE]) if emit_state else None
    return x, state


def kernel(x_prompt, x_sample, cache_diff_k, cache_diff_v, cache_gqa_k, cache_gqa_v, cache_mla_ckv, cache_mla_kpe, c, c_ctx, w_mod, b_mod, norm_ffn1, norm_mix, norm_ffn2, ffn1_w_in, ffn1_w_out, ffn2_w_in, ffn2_w_out, w_in, w_out, diff_lq1, diff_lk1, diff_lq2, diff_lk2, diff_subln, gqa_q_norm, gqa_k_norm, mla_q_norm, mla_kv_norm, mla_w_uq, mla_w_ukv, final_norm):
    batch, seq, d = x_prompt.shape
    dec_batch, dec_seq, _ = x_sample.shape
    depth = w_mod.shape[0]
    past = cache_diff_k.shape[2]
    diff_heads = cache_diff_k.shape[3]
    kv_heads = cache_gqa_k.shape[3]
    q_lora, kv_lora = mla_q_norm.shape[-1], mla_kv_norm.shape[-1]
    mla_heads = mla_w_ukv.shape[-1] // (MLA_NOPE + MLA_V)
    n_dqk = 2 * diff_heads * 2 * DIFF_QK
    n_dv = diff_heads * DIFF_V
    n_gk = kv_heads * HEAD_DIM
    n_gq = w_in.shape[-1] - (n_dqk + n_dv + 2 * n_gk + q_lora + kv_lora + MLA_ROPE)
    dims = (n_dqk, n_dv, n_gq, n_gk, q_lora, kv_lora)
    assert dec_batch + 1 <= MOD_ROWS

    cond = jnp.zeros((MOD_ROWS, d), F32).at[0].set(c_ctx).at[1:1 + dec_batch].set(c)
    mod4 = _modulation(cond, w_mod, b_mod).reshape(depth, MOD_ROWS, 1, N_MOD * d)

    uq = mla_w_uq.reshape(depth, q_lora, mla_heads, MLA_NOPE + MLA_ROPE)
    uq_pe = jnp.pad(uq[..., MLA_NOPE:], ((0, 0), (0, 0), (0, 0), (0, LANES - MLA_ROPE)))
    uq = jnp.concatenate([uq[..., :MLA_NOPE].reshape(depth, q_lora, -1), uq_pe.reshape(depth, q_lora, -1)], axis=-1)
    cache_kpe = jnp.pad(cache_mla_kpe, ((0, 0), (0, 0), (0, 0), (0, LANES - MLA_ROPE)))
    w_in_t = jnp.swapaxes(w_in, 1, 2)

    r3 = lambda a: a.reshape(depth, 1, a.shape[-1])
    weights = {
        "ffn1_w_in": ffn1_w_in, "ffn1_w_out": ffn1_w_out, "ffn2_w_in": ffn2_w_in, "ffn2_w_out": ffn2_w_out,
        "w_in_t": w_in_t, "w_out": w_out, "mla_w_uq": uq, "mla_w_ukv": mla_w_ukv,
        "norm_ffn1": r3(norm_ffn1), "norm_mix": r3(norm_mix), "norm_ffn2": r3(norm_ffn2),
        "gqa_q_norm": r3(gqa_q_norm), "gqa_k_norm": r3(gqa_k_norm),
        "mla_q_norm": r3(mla_q_norm), "mla_kv_norm": r3(mla_kv_norm),
        "diff_l": (r3(diff_lq1), r3(diff_lk1), r3(diff_lq2), r3(diff_lk2)), "diff_subln": r3(diff_subln),
        "dims": dims,
    }
    caches = {
        "diff_k": cache_diff_k.reshape(dec_batch, depth, past, -1),
        "diff_v": cache_diff_v.reshape(dec_batch, depth, past, -1),
        "gqa_k": cache_gqa_k.reshape(dec_batch, depth, past, -1),
        "gqa_v": cache_gqa_v.reshape(dec_batch, depth, past, -1),
        "mla_ckv": cache_mla_ckv, "mla_kpe": cache_kpe,
    }
    tables = (_rope_tables(dec_seq, HEAD_DIM), _rope_tables(dec_seq, DIFF_QK))

    h = x_prompt.reshape(batch * seq, d)
    state, kpes = None, []
    for l in range(depth):
        h, (state, kpe_l) = _layer(h, l, weights, mod4, (0, batch * seq), batch, seq, None, None, True, state)
        kpes.append(kpe_l.reshape(batch, seq, MLA_ROPE))
    y_prompt = _final_norm(h, final_norm).reshape(batch, seq, d)

    h = x_sample.reshape(dec_batch * dec_seq, d)
    for l in range(depth):
        h, _ = _layer(h, l, weights, mod4, (1, dec_seq), dec_batch, dec_seq, caches, tables, False)
    y_sample = _final_norm(h, final_norm).reshape(dec_batch, dec_seq, d)

    dk, dv, gk, gv, ckv = state
    lead = (batch, depth, seq)
    return (y_prompt, y_sample,
            dk.reshape(lead + (diff_heads, 2, DIFF_QK)), dv.reshape(lead + (diff_heads, DIFF_V)),
            gk.reshape(lead + (kv_heads, HEAD_DIM)), gv.reshape(lead + (kv_heads, HEAD_DIM)),
            ckv, jnp.stack(kpes, axis=1))
```

```python
import functools
import math
import operator

import jax
import jax.numpy as jnp
from jax import lax
from jax.experimental import pallas as pl
from jax.experimental.pallas import tpu as pltpu

F32 = jnp.float32
BF16 = jnp.bfloat16

NORM_EPS = 1e-6
ROPE_THETA = 10000.0
GRID_W = 64
N_MOD = 9
DIFF_QK = 64
DIFF_V = 128
HEAD_DIM = 128
MLA_NOPE = 128
MLA_ROPE = 64
MLA_V = 128
LANES = 128
MOD_ROWS = 16
VMEM_LIMIT_BYTES = 56 * 1024 * 1024
W_TILE_BYTES_MAX = 6 * 1024 * 1024
LOG2_E = 1.4426950408889634


def _params(sem):
    return pltpu.CompilerParams(dimension_semantics=sem, vmem_limit_bytes=VMEM_LIMIT_BYTES)


def _silu(x):
    return x / (1.0 + jnp.exp(-x))


def _rms(x, gain):
    y = x * lax.rsqrt(jnp.mean(x * x, axis=-1, keepdims=True) + NORM_EPS)
    return y * gain


def _rope(x, cos, sin_lo, sin_hi, shift):
    return (x * cos + pltpu.roll(x, LANES - shift, 1) * sin_lo + pltpu.roll(x, shift, 1) * sin_hi)


_NT = (((1,), (1,)), ((), ()))


def _dot_nt(a, b):
    return lax.dot_general(a, b, _NT, preferred_element_type=F32)


def _sum(xs):
    return functools.reduce(operator.add, xs)


def _mod_kernel(c_ref, w_ref, b_ref, o_ref):
    a = _silu(c_ref[...]).astype(BF16)
    o_ref[...] = jnp.dot(a, w_ref[...].astype(BF16), preferred_element_type=F32) + b_ref[...]


def _modulation(cond, w_mod, b_mod):
    depth, d, n = w_mod.shape
    tn = 512 if n % 512 == 0 else n
    return pl.pallas_call(
        _mod_kernel,
        grid=(depth, n // tn),
        in_specs=[
            pl.BlockSpec((MOD_ROWS, d), lambda l, j: (0, 0)),
            pl.BlockSpec((None, d, tn), lambda l, j: (l, 0, j)),
            pl.BlockSpec((None, 1, tn), lambda l, j: (l, 0, j)),
        ],
        out_specs=pl.BlockSpec((None, MOD_ROWS, tn), lambda l, j: (l, 0, j)),
        out_shape=jax.ShapeDtypeStruct((depth, MOD_ROWS, n), F32),
        compiler_params=_params(("arbitrary", "arbitrary")),
        name="modulation",
    )(cond, w_mod, b_mod.reshape(depth, 1, n))


NORM_ROWS = 16


def _norm_rows(x_ref, o_ref, gain, shift):
    def body(r, carry):
        rows = pl.ds(pl.multiple_of(r * NORM_ROWS, NORM_ROWS), NORM_ROWS)
        x = x_ref[rows, :]
        y = x * lax.rsqrt(jnp.mean(x * x, axis=-1, keepdims=True) + NORM_EPS) * gain
        o_ref[rows, :] = (y if shift is None else y + shift).astype(o_ref.dtype)
        return carry

    lax.fori_loop(0, x_ref.shape[0] // NORM_ROWS, body, 0, unroll=8)


def _normmod_kernel(x_ref, g_ref, sh_ref, sc_ref, o_ref):
    _norm_rows(x_ref, o_ref, g_ref[...] * (1.0 + sc_ref[...]), sh_ref[...])


def _norm_kernel(x_ref, g_ref, o_ref):
    _norm_rows(x_ref, o_ref, g_ref[...], None)


def _row_tile(m, target):
    t = min(m, target)
    assert m % t == 0
    return t


def _normmod(x, gain3, l, mod4, grp, chunk_shift, chunk_scale):
    m, d = x.shape
    g_base, g_rows = grp
    tr = _row_tile(g_rows, 512)
    return pl.pallas_call(
        _normmod_kernel,
        grid=(m // tr,),
        in_specs=[
            pl.BlockSpec((tr, d), lambda i: (i, 0)),
            pl.BlockSpec((None, 1, d), lambda i: (l, 0, 0)),
            pl.BlockSpec((None, None, 1, d), lambda i: (l, g_base + (i * tr) // g_rows, 0, chunk_shift)),
            pl.BlockSpec((None, None, 1, d), lambda i: (l, g_base + (i * tr) // g_rows, 0, chunk_scale)),
        ],
        out_specs=pl.BlockSpec((tr, d), lambda i: (i, 0)),
        out_shape=jax.ShapeDtypeStruct((m, d), BF16),
        compiler_params=_params(("arbitrary",)),
        name="normmod",
    )(x, gain3, mod4, mod4)


def _final_norm(x, gain):
    m, d = x.shape
    tr = _row_tile(m, 512)
    return pl.pallas_call(
        _norm_kernel,
        grid=(m // tr,),
        in_specs=[pl.BlockSpec((tr, d), lambda i: (i, 0)), pl.BlockSpec((1, d), lambda i: (0, 0))],
        out_specs=pl.BlockSpec((tr, d), lambda i: (i, 0)),
        out_shape=jax.ShapeDtypeStruct((m, d), F32),
        compiler_params=_params(("arbitrary",)),
        name="final_norm",
    )(x, gain.reshape(1, d))


def _mm_kernel(a_ref, w_ref, o_ref):
    a = a_ref[...].astype(BF16)
    o_ref[...] = jnp.dot(a, w_ref[...].astype(BF16), preferred_element_type=F32).astype(o_ref.dtype)


def _mm_t_kernel(a_ref, wt_ref, o_ref):
    o_ref[...] = _dot_nt(a_ref[...], wt_ref[...].astype(BF16)).astype(o_ref.dtype)


def _mm_t_tail_kernel(a_ref, wt_ref, o_ref, *, valid):
    wt = wt_ref[...]
    row = lax.broadcasted_iota(jnp.int32, wt.shape, 0)
    wt = jnp.where(row < valid, wt, 0.0).astype(BF16)
    o_ref[...] = _dot_nt(a_ref[...], wt)


def _mm_swiglu_kernel(a_ref, wg_ref, wu_ref, o_ref):
    a = a_ref[...]
    g = jnp.dot(a, wg_ref[...].astype(BF16), preferred_element_type=F32)
    u = jnp.dot(a, wu_ref[...].astype(BF16), preferred_element_type=F32)
    o_ref[...] = (_silu(g) * u).astype(o_ref.dtype)


def _mm_resid_kernel(*refs, widths, coef, nk, tk):
    n_a = len(widths)
    a_refs = refs[:n_a]
    w_ref, x_ref, gate_ref, o_ref, acc_ref = refs[n_a:]
    if nk == 1:
        row = 0
        parts = []
        for a_ref, width in zip(a_refs, widths):
            parts.append(jnp.dot(a_ref[...], w_ref[row:row + width, :].astype(BF16), preferred_element_type=F32))
            row += width
        o_ref[...] = x_ref[...] + (coef * gate_ref[...]) * _sum(parts)
        return
    k = pl.program_id(2)
    a_ref, = a_refs
    depth_k = a_ref.shape[1]
    for kk in range(nk):
        @pl.when(k == kk)
        def _(kk=kk):
            rows = min(tk, depth_k - kk * tk)
            part = jnp.dot(a_ref[:, kk * tk:kk * tk + rows], w_ref[0:rows, :].astype(BF16),
                           preferred_element_type=F32)
            if kk > 0:
                part = acc_ref[...] + part
            if kk == nk - 1:
                o_ref[...] = x_ref[...] + (coef * gate_ref[...]) * part
            else:
                acc_ref[...] = part


def _col_tile(n, target=256):
    return target if n % target == 0 else n


def _matmul_t_tail(a, wt, l, row0, name):
    m, k = a.shape
    valid = wt.shape[1] - row0
    assert row0 % LANES == 0 and 0 < valid <= LANES
    tm = _row_tile(m, 1024)
    return pl.pallas_call(
        functools.partial(_mm_t_tail_kernel, valid=valid),
        grid=(m // tm,),
        in_specs=[
            pl.BlockSpec((tm, k), lambda i: (i, 0)),
            pl.BlockSpec((None, LANES, k), lambda i: (l, row0 // LANES, 0)),
        ],
        out_specs=pl.BlockSpec((tm, LANES), lambda i: (i, 0)),
        out_shape=jax.ShapeDtypeStruct((m, LANES), F32),
        compiler_params=_params(("arbitrary",)),
        name=name,
    )(a, wt)


def _matmul_t(a, wt, l, out_dtype, name, n, tn_target=256):
    m, k = a.shape
    tm = _row_tile(m, 1024)
    tn = _col_tile(n, tn_target)
    return pl.pallas_call(
        _mm_t_kernel,
        grid=(m // tm, n // tn),
        in_specs=[
            pl.BlockSpec((tm, k), lambda i, j: (i, 0)),
            pl.BlockSpec((None, tn, k), lambda i, j: (l, j, 0)),
        ],
        out_specs=pl.BlockSpec((tm, tn), lambda i, j: (i, j)),
        out_shape=jax.ShapeDtypeStruct((m, n), out_dtype),
        compiler_params=_params(("arbitrary", "arbitrary")),
        name=name,
    )(a, wt)


def _matmul(a, w, l, out_dtype, name, n=None, tn_target=256):
    m, k = a.shape
    n = w.shape[-1] if n is None else n
    tm = _row_tile(m, 1024)
    tn = _col_tile(n, tn_target)
    return pl.pallas_call(
        _mm_kernel,
        grid=(m // tm, n // tn),
        in_specs=[
            pl.BlockSpec((tm, k), lambda i, j: (i, 0)),
            pl.BlockSpec((None, k, tn), lambda i, j: (l, 0, j)),
        ],
        out_specs=pl.BlockSpec((tm, tn), lambda i, j: (i, j)),
        out_shape=jax.ShapeDtypeStruct((m, n), out_dtype),
        compiler_params=_params(("arbitrary", "arbitrary")),
        name=name,
    )(a, w)


def _matmul_cache(a4, l, w, lw, out_dtype, name):
    b, _, s, k = a4.shape
    n = w.shape[-1]
    tn = n
    return pl.pallas_call(
        _mm_kernel,
        grid=(b, n // tn),
        in_specs=[
            pl.BlockSpec((None, None, s, k), lambda i, j: (i, l, 0, 0)),
            pl.BlockSpec((None, k, tn), lambda i, j: (lw, 0, j)),
        ],
        out_specs=pl.BlockSpec((s, tn), lambda i, j: (i, j)),
        out_shape=jax.ShapeDtypeStruct((b * s, n), out_dtype),
        compiler_params=_params(("arbitrary", "arbitrary")),
        name=name,
    )(a4, w)


def _matmul_swiglu(a, w, l):
    m, k = a.shape
    f = w.shape[-1] // 2
    tm = _row_tile(m, 2048)
    tn = _col_tile(f)
    nj = f // tn
    return pl.pallas_call(
        _mm_swiglu_kernel,
        grid=(m // tm, nj),
        in_specs=[
            pl.BlockSpec((tm, k), lambda i, j: (i, 0), pipeline_mode=pl.Buffered(1)),
            pl.BlockSpec((None, k, tn), lambda i, j: (l, 0, j)),
            pl.BlockSpec((None, k, tn), lambda i, j: (l, 0, j + nj)),
        ],
        out_specs=pl.BlockSpec((tm, tn), lambda i, j: (i, j)),
        out_shape=jax.ShapeDtypeStruct((m, f), BF16),
        compiler_params=_params(("arbitrary", "arbitrary")),
        name="ffn_in",
    )(a, w, w)


def _matmul_resid(a_parts, w, l, x, mod4, grp, chunk, coef, name, tn_target=256):
    m = a_parts[0].shape[0]
    widths = tuple(a.shape[1] for a in a_parts)
    k = sum(widths)
    n = w.shape[-1]
    g_base, g_rows = grp
    tm = _row_tile(g_rows, 1024)
    tn = _col_tile(n, tn_target)
    nk, tk = 1, k
    while len(a_parts) == 1 and tk * tn * 4 > W_TILE_BYTES_MAX:
        nk += 1
        tk = pl.cdiv(pl.cdiv(k, nk), LANES) * LANES
    assert (nk - 1) * tk < k
    per_chunk = n // tn
    kern = functools.partial(_mm_resid_kernel, widths=widths, coef=coef, nk=nk, tk=tk)
    a_mode = {"pipeline_mode": pl.Buffered(1)} if nk > 1 else {}
    return pl.pallas_call(
        kern,
        grid=(m // tm, n // tn, nk),
        in_specs=[pl.BlockSpec((tm, width), lambda i, j, kk: (i, 0), **a_mode) for width in widths] + [
            pl.BlockSpec((None, tk, tn), lambda i, j, kk: (l, kk, j)),
            pl.BlockSpec((tm, tn), lambda i, j, kk: (i, j)),
            pl.BlockSpec((None, None, 1, tn), lambda i, j, kk: (l, g_base + (i * tm) // g_rows, 0, chunk * per_chunk + j)),
        ],
        out_specs=pl.BlockSpec((tm, tn), lambda i, j, kk: (i, j)),
        out_shape=jax.ShapeDtypeStruct((m, n), F32),
        scratch_shapes=[pltpu.VMEM((tm, tn), F32)],
        compiler_params=_params(("arbitrary", "arbitrary", "arbitrary")),
        name=name,
    )(*a_parts, w, x, mod4)


def _post_kernel(*refs, dims, rope, emit_state, n_alias):
    n_dqk, n_dv, n_gq, n_gk, q_lora, kv_lora = dims
    p_ref, pk_ref, gqn_ref, gkn_ref, mqn_ref, mkvn_ref = refs[:6]
    pos = 6
    if rope:
        c128, lo128, hi128, c64, lo64, hi64 = (r[...] for r in refs[pos:pos + 6])
        pos += 6
    pos += n_alias
    qkv_ref, cq_ref, ckv_ref, kpe_ref = refs[pos:pos + 4]
    pos += 4
    if emit_state:
        dk_f_ref, dv_f_ref, gk_f_ref, gv_f_ref, ckv_f_ref = refs[pos:pos + 5]

    col = 0
    for _ in range(n_dqk // LANES):
        x = p_ref[:, col:col + LANES]
        if rope:
            x = _rope(x, c64, lo64, hi64, DIFF_QK // 4)
        qkv_ref[:, col:col + LANES] = x.astype(BF16)
        col += LANES
    if emit_state:
        dk_f_ref[...] = p_ref[:, n_dqk // 2:n_dqk]
    dv = p_ref[:, col:col + n_dv]
    qkv_ref[:, col:col + n_dv] = dv.astype(BF16)
    if emit_state:
        dv_f_ref[...] = dv
    col += n_dv
    for _ in range(n_gq // LANES):
        x = _rms(p_ref[:, col:col + LANES], gqn_ref[...])
        if rope:
            x = _rope(x, c128, lo128, hi128, HEAD_DIM // 4)
        qkv_ref[:, col:col + LANES] = x.astype(BF16)
        col += LANES
    for h in range(n_gk // LANES):
        x = _rms(p_ref[:, col:col + LANES], gkn_ref[...])
        if emit_state:
            gk_f_ref[:, h * LANES:(h + 1) * LANES] = x
        if rope:
            x = _rope(x, c128, lo128, hi128, HEAD_DIM // 4)
        qkv_ref[:, col:col + LANES] = x.astype(BF16)
        col += LANES
    gv = p_ref[:, col:col + n_gk]
    qkv_ref[:, col:col + n_gk] = gv.astype(BF16)
    if emit_state:
        gv_f_ref[...] = gv
    col += n_gk
    cq_ref[...] = _rms(p_ref[:, col:col + q_lora], mqn_ref[...]).astype(BF16)
    col += q_lora
    ckv = _rms(p_ref[:, col:col + kv_lora], mkvn_ref[...])
    ckv_ref[...] = ckv.astype(BF16)
    if emit_state:
        ckv_f_ref[...] = ckv
    kpe = pk_ref[...]
    if rope:
        kpe = _rope(kpe, c64, lo64, hi64, MLA_ROPE // 4)
    kpe_ref[...] = kpe.astype(BF16)


def _post(p, pk, l, depth, dims, gqn, gkn, mqn, mkvn, tables, t_len, emit_state, state_in):
    m, n = p.shape
    n_dqk, n_dv, n_gq, n_gk, q_lora, kv_lora = dims
    n_qkv = n_dqk + n_dv + n_gq + 2 * n_gk
    rope = tables is not None
    tr = _row_tile(t_len, 256)
    nt = t_len // tr
    row_spec = lambda width: pl.BlockSpec((tr, width), lambda i: (i, 0))
    gain_spec = lambda width: pl.BlockSpec((None, 1, width), lambda i: (l, 0, 0))
    in_specs = [row_spec(n), row_spec(LANES), gain_spec(HEAD_DIM), gain_spec(HEAD_DIM), gain_spec(q_lora),
                gain_spec(kv_lora)]
    args = [p, pk, gqn, gkn, mqn, mkvn]
    if rope:
        in_specs += [pl.BlockSpec((tr, LANES), lambda i: (i % nt, 0))] * 6
        args += list(tables)
    out_widths = [(n_qkv, BF16), (q_lora, BF16), (kv_lora, BF16), (LANES, BF16)]
    out_specs = [row_spec(width) for width, _ in out_widths]
    out_shape = [jax.ShapeDtypeStruct((m, width), dt) for width, dt in out_widths]
    aliases = {}
    n_alias = 0
    if emit_state:
        state_widths = [n_dqk // 2, n_dv, n_gk, n_gk, kv_lora]
        if state_in is not None:
            n_alias = len(state_in)
            aliases = {len(args) + s: len(out_shape) + s for s in range(n_alias)}
            in_specs += [pl.BlockSpec(memory_space=pl.ANY)] * n_alias
            args += list(state_in)
        out_specs += [pl.BlockSpec((None, None, tr, width), lambda i: (i // nt, l, i % nt, 0)) for width in state_widths]
        out_shape += [jax.ShapeDtypeStruct((m // t_len, depth, t_len, width), F32) for width in state_widths]
    kern = functools.partial(_post_kernel, dims=dims, rope=rope, emit_state=emit_state, n_alias=n_alias)
    return pl.pallas_call(
        kern,
        grid=(m // tr,),
        in_specs=in_specs,
        out_specs=out_specs,
        out_shape=out_shape,
        input_output_aliases=aliases,
        compiler_params=_params(("arbitrary",)),
        name="mixer_post",
    )(*args)


def _attend(q_parts, segments, c):
    s = [_sum([_dot_nt(q, k) for q, k in zip(q_parts, ks)]) for ks, _ in segments]
    m = functools.reduce(jnp.maximum, [jnp.max(x, axis=-1, keepdims=True) for x in s])
    e = [jnp.exp2((x - m) * c) for x in s]
    den = _sum([jnp.sum(x, axis=-1, keepdims=True) for x in e])
    o = _sum([jnp.dot(x.astype(BF16), v, preferred_element_type=F32) for x, (_, v) in zip(e, segments)])
    return o / den


def _diff_attn_kernel(*refs, heads, has_cache, lam_init):
    q_ref, k_ref, v_ref = refs[:3]
    pos = 3
    if has_cache:
        kc_ref, vc_ref = refs[pos:pos + 2]
        pos += 2
    lq1, lk1, lq2, lk2, sub_ref, o_ref = refs[pos:pos + 6]
    lam = (jnp.exp(jnp.sum(lq1[...] * lk1[...], axis=-1, keepdims=True))
           - jnp.exp(jnp.sum(lq2[...] * lk2[...], axis=-1, keepdims=True)) + lam_init)
    c = DIFF_QK ** -0.5 * LOG2_E
    lane = lax.broadcasted_iota(jnp.int32, (q_ref.shape[0], LANES), 1)
    for h in range(heads):
        cols = slice(h * LANES, (h + 1) * LANES)
        q = q_ref[:, cols]
        zero = jnp.zeros_like(q)
        segments = [([k_ref[:, cols]], v_ref[:, cols])]
        if has_cache:
            segments.insert(0, ([kc_ref[:, cols].astype(BF16)], vc_ref[:, cols].astype(BF16)))
        o0 = _attend([jnp.where(lane < DIFF_QK, q, zero)], segments, c)
        o1 = _attend([jnp.where(lane >= DIFF_QK, q, zero)], segments, c)
        o = o0 - lam * o1
        o_ref[:, cols] = (_rms(o, sub_ref[...]) * (1.0 - lam_init)).astype(o_ref.dtype)


def _gqa_attn_kernel(*refs, kv_heads, group, has_cache):
    q_ref, k_ref, v_ref = refs[:3]
    pos = 3
    if has_cache:
        kc_ref, vc_ref = refs[pos:pos + 2]
        pos += 2
    o_ref = refs[pos]
    c = HEAD_DIM ** -0.5 * LOG2_E
    for kvh in range(kv_heads):
        cols = slice(kvh * LANES, (kvh + 1) * LANES)
        segments = [([k_ref[:, cols]], v_ref[:, cols])]
        if has_cache:
            segments.insert(0, ([kc_ref[:, cols].astype(BF16)], vc_ref[:, cols].astype(BF16)))
        for g in range(group):
            qcols = slice((kvh * group + g) * LANES, (kvh * group + g + 1) * LANES)
            o_ref[:, qcols] = _attend([q_ref[:, qcols]], segments, c).astype(o_ref.dtype)


def _mla_attn_kernel(*refs, heads, has_cache, rope):
    mq_ref, kv_ref, kp_ref = refs[:3]
    pos = 3
    if has_cache:
        kvc_ref, kpc_ref = refs[pos:pos + 2]
        pos += 2
    if rope:
        c64, lo64, hi64 = (r[...] for r in refs[pos:pos + 3])
        pos += 3
    o_ref = refs[pos]
    c = (MLA_NOPE + MLA_ROPE) ** -0.5 * LOG2_E
    kp = kp_ref[...]
    if has_cache:
        kpc = kpc_ref[...].astype(BF16)
    for h in range(heads):
        cols = slice(h * LANES, (h + 1) * LANES)
        kcols = slice(2 * h * LANES, (2 * h + 1) * LANES)
        vcols = slice((2 * h + 1) * LANES, (2 * h + 2) * LANES)
        qn = mq_ref[:, cols].astype(BF16)
        qp = mq_ref[:, (heads + h) * LANES:(heads + h + 1) * LANES]
        if rope:
            qp = _rope(qp, c64, lo64, hi64, MLA_ROPE // 4)
        q = jnp.concatenate([qn, qp.astype(BF16)], axis=-1)
        segments = [([jnp.concatenate([kv_ref[:, kcols], kp], axis=-1)], kv_ref[:, vcols])]
        if has_cache:
            segments.insert(0, ([jnp.concatenate([kvc_ref[:, kcols], kpc], axis=-1)], kvc_ref[:, vcols]))
        o_ref[:, cols] = _attend([q], segments, c).astype(o_ref.dtype)


def _q_tile(t):
    return 256 if t % 256 == 0 else t


def _attn_call(kern, name, batch, t, out_width, q_specs, kv_specs, cache_specs, extra_specs, args):
    tq = _q_tile(t)
    nq = t // tq
    in_specs = ([pl.BlockSpec((tq, w), lambda b, i, c=c: (b * nq + i, c)) for w, c in q_specs]
                + [pl.BlockSpec((t, w), lambda b, i, c=c: (b, c)) for w, c in kv_specs]
                + cache_specs + extra_specs)
    return pl.pallas_call(
        kern,
        grid=(batch, nq),
        in_specs=in_specs,
        out_specs=pl.BlockSpec((tq, out_width), lambda b, i: (b * nq + i, 0)),
        out_shape=jax.ShapeDtypeStruct((batch * t, out_width), BF16),
        compiler_params=_params(("arbitrary", "arbitrary")),
        name=name,
    )(*args)


def _cache_spec(l, s_cache, width, col):
    return pl.BlockSpec((None, None, s_cache, width), lambda b, i: (b, l, 0, col))


def _diff_attention(qkv, l, batch, t, heads, cache_k4, cache_v4, lparams, subln):
    has_cache = cache_k4 is not None
    w = heads * LANES
    cache_specs, args = [], [qkv, qkv, qkv]
    if has_cache:
        cache_specs = [_cache_spec(l, cache_k4.shape[2], w, 0)] * 2
        args += [cache_k4, cache_v4]
    extra = [pl.BlockSpec((None, 1, DIFF_QK), lambda b, i: (l, 0, 0))] * 4
    extra += [pl.BlockSpec((None, 1, DIFF_V), lambda b, i: (l, 0, 0))]
    args += list(lparams) + [subln]
    lam_init = 0.8 - 0.6 * math.exp(-0.3 * l)
    kern = functools.partial(_diff_attn_kernel, heads=heads, has_cache=has_cache, lam_init=lam_init)
    return _attn_call(kern, "diff_attention", batch, t, w, [(w, 0)], [(w, 1), (w, 2)], cache_specs, extra, args)


def _gqa_attention(qkv, l, batch, t, col0, q_heads, kv_heads, cache_k4, cache_v4):
    has_cache = cache_k4 is not None
    qw, kw = q_heads * HEAD_DIM, kv_heads * HEAD_DIM
    assert col0 % qw == 0 and (col0 + qw) % kw == 0
    k_blk = (col0 + qw) // kw
    cache_specs, args = [], [qkv, qkv, qkv]
    if has_cache:
        cache_specs = [_cache_spec(l, cache_k4.shape[2], kw, 0)] * 2
        args += [cache_k4, cache_v4]
    kern = functools.partial(_gqa_attn_kernel, kv_heads=kv_heads, group=q_heads // kv_heads, has_cache=has_cache)
    return _attn_call(kern, "gqa_attention", batch, t, qw, [(qw, col0 // qw)], [(kw, k_blk), (kw, k_blk + 1)],
                      cache_specs, [], args)


def _mla_attention(mq, kv, kpe, l, batch, t, heads, kv_cache, cache_kpe4, tables64):
    has_cache = kv_cache is not None
    rope = tables64 is not None
    tq = _q_tile(t)
    cache_specs, extra, args = [], [], [mq, kv, kpe]
    if has_cache:
        s_cache = cache_kpe4.shape[2]
        cache_specs = [pl.BlockSpec((s_cache, kv.shape[1]), lambda b, i: (b, 0)), _cache_spec(l, s_cache, LANES, 0)]
        args += [kv_cache, cache_kpe4]
    if rope:
        extra = [pl.BlockSpec((tq, LANES), lambda b, i: (i, 0))] * 3
        args += list(tables64)
    kern = functools.partial(_mla_attn_kernel, heads=heads, has_cache=has_cache, rope=rope)
    return _attn_call(kern, "mla_attention", batch, t, heads * MLA_V, [(mq.shape[1], 0)],
                      [(kv.shape[1], 0), (LANES, 0)], cache_specs, extra, args)


def _rope_tables(t_len, width):
    half = width // 2
    quarter = half // 2
    t = jnp.arange(t_len, dtype=jnp.int32)
    rows = (t // GRID_W).astype(F32)
    cols = (t % GRID_W).astype(F32)
    lane = jnp.arange(LANES, dtype=jnp.int32) % width
    inv = ROPE_THETA ** (-(jnp.arange(0, half, 2, dtype=F32) / half))
    inv_lane = inv[(lane % half) % quarter]
    pos = jnp.where((lane < half)[None, :], rows[:, None], cols[:, None])
    ang = pos * inv_lane[None, :]
    cos, sin = jnp.cos(ang), jnp.sin(ang)
    first = ((lane % half) < quarter)[None, :]
    zero = jnp.zeros_like(sin)
    return cos, jnp.where(first, -sin, zero), jnp.where(first, zero, sin)


def _layer(x, l, w, mod4, grp, batch, t, caches, tables, emit_state, state_in=None):
    ffn = lambda h, w_in, w_out, norm, c0, name: _matmul_resid(
        [_matmul_swiglu(_normmod(h, norm, l, mod4, grp, c0, c0 + 1), w_in, l)],
        w_out, l, h, mod4, grp, c0 + 2, 0.5, name, tn_target=512)
    x = ffn(x, w["ffn1_w_in"], w["ffn1_w_out"], w["norm_ffn1"], 0, "ffn1_out")

    u = _normmod(x, w["norm_mix"], l, mod4, grp, 3, 4)
    dims = w["dims"]
    n_dqk, n_dv, n_gq, n_gk, q_lora, kv_lora = dims
    n_main = n_dqk + n_dv + n_gq + 2 * n_gk + q_lora + kv_lora
    p = _matmul_t(u, w["w_in_t"], l, F32, "mixer_in", n_main, tn_target=512)
    pk = _matmul_t_tail(u, w["w_in_t"], l, n_main, "mixer_in_kpe")
    post = _post(p, pk, l, w["ffn1_w_in"].shape[0], dims, w["gqa_q_norm"], w["gqa_k_norm"], w["mla_q_norm"],
                 w["mla_kv_norm"], tables[0] + tables[1] if tables is not None else None, t, emit_state, state_in)
    qkv, cqn, ckvn, kpe = post[:4]

    diff_heads = n_dv // DIFF_V
    q_heads, kv_heads = n_gq // HEAD_DIM, n_gk // HEAD_DIM
    mla_heads = w["mla_w_ukv"].shape[-1] // (MLA_NOPE + MLA_V)
    has_cache = caches is not None
    o_diff = _diff_attention(qkv, l, batch, t, diff_heads,
                             caches["diff_k"] if has_cache else None, caches["diff_v"] if has_cache else None,
                             w["diff_l"], w["diff_subln"])
    o_gqa = _gqa_attention(qkv, l, batch, t, n_dqk + n_dv, q_heads, kv_heads,
                           caches["gqa_k"] if has_cache else None, caches["gqa_v"] if has_cache else None)
    mq = _matmul(cqn, w["mla_w_uq"], l, F32, "mla_q_up", tn_target=1024)
    kv = _matmul(ckvn, w["mla_w_ukv"], l, BF16, "mla_kv_up", tn_target=w["mla_w_ukv"].shape[-1])
    kv_cache = _matmul_cache(caches["mla_ckv"], l, w["mla_w_ukv"], l, BF16, "mla_kv_up_cache") if has_cache else None
    o_mla = _mla_attention(mq, kv, kpe, l, batch, t, mla_heads, kv_cache,
                           caches["mla_kpe"] if has_cache else None, tables[1] if tables is not None else None)
    x = _matmul_resid([o_diff, o_gqa, o_mla], w["w_out"], l, x, mod4, grp, 5, 1.0, "mixer_out", tn_target=512)

    x = ffn(x, w["ffn2_w_in"], w["ffn2_w_out"], w["norm_ffn2"], 6, "ffn2_out")
    state = (list(post[4:]), pk[:, :MLA_ROPE]) if emit_state else None
    return x, state


def kernel(x_prompt, x_sample, cache_diff_k, cache_diff_v, cache_gqa_k, cache_gqa_v, cache_mla_ckv, cache_mla_kpe, c, c_ctx, w_mod, b_mod, norm_ffn1, norm_mix, norm_ffn2, ffn1_w_in, ffn1_w_out, ffn2_w_in, ffn2_w_out, w_in, w_out, diff_lq1, diff_lk1, diff_lq2, diff_lk2, diff_subln, gqa_q_norm, gqa_k_norm, mla_q_norm, mla_kv_norm, mla_w_uq, mla_w_ukv, final_norm):
    batch, seq, d = x_prompt.shape
    dec_batch, dec_seq, _ = x_sample.shape
    depth = w_mod.shape[0]
    past = cache_diff_k.shape[2]
    diff_heads = cache_diff_k.shape[3]
    kv_heads = cache_gqa_k.shape[3]
    q_lora, kv_lora = mla_q_norm.shape[-1], mla_kv_norm.shape[-1]
    mla_heads = mla_w_ukv.shape[-1] // (MLA_NOPE + MLA_V)
    n_dqk = 2 * diff_heads * 2 * DIFF_QK
    n_dv = diff_heads * DIFF_V
    n_gk = kv_heads * HEAD_DIM
    n_gq = w_in.shape[-1] - (n_dqk + n_dv + 2 * n_gk + q_lora + kv_lora + MLA_ROPE)
    dims = (n_dqk, n_dv, n_gq, n_gk, q_lora, kv_lora)
    assert dec_batch + 1 <= MOD_ROWS

    cond = jnp.zeros((MOD_ROWS, d), F32).at[0].set(c_ctx).at[1:1 + dec_batch].set(c)
    mod4 = _modulation(cond, w_mod, b_mod).reshape(depth, MOD_ROWS, 1, N_MOD * d)

    uq = mla_w_uq.reshape(depth, q_lora, mla_heads, MLA_NOPE + MLA_ROPE)
    uq_pe = jnp.pad(uq[..., MLA_NOPE:], ((0, 0), (0, 0), (0, 0), (0, LANES - MLA_ROPE)))
    uq = jnp.concatenate([uq[..., :MLA_NOPE].reshape(depth, q_lora, -1), uq_pe.reshape(depth, q_lora, -1)], axis=-1)
    cache_kpe = jnp.pad(cache_mla_kpe, ((0, 0), (0, 0), (0, 0), (0, LANES - MLA_ROPE)))
    w_in_t = jnp.swapaxes(w_in, 1, 2)

    r3 = lambda a: a.reshape(depth, 1, a.shape[-1])
    weights = {
        "ffn1_w_in": ffn1_w_in, "ffn1_w_out": ffn1_w_out, "ffn2_w_in": ffn2_w_in, "ffn2_w_out": ffn2_w_out,
        "w_in_t": w_in_t, "w_out": w_out, "mla_w_uq": uq, "mla_w_ukv": mla_w_ukv,
        "norm_ffn1": r3(norm_ffn1), "norm_mix": r3(norm_mix), "norm_ffn2": r3(norm_ffn2),
        "gqa_q_norm": r3(gqa_q_norm), "gqa_k_norm": r3(gqa_k_norm),
        "mla_q_norm": r3(mla_q_norm), "mla_kv_norm": r3(mla_kv_norm),
        "diff_l": (r3(diff_lq1), r3(diff_lk1), r3(diff_lq2), r3(diff_lk2)), "diff_subln": r3(diff_subln),
        "dims": dims,
    }
    caches = {
        "diff_k": cache_diff_k.reshape(dec_batch, depth, past, -1),
        "diff_v": cache_diff_v.reshape(dec_batch, depth, past, -1),
        "gqa_k": cache_gqa_k.reshape(dec_batch, depth, past, -1),
        "gqa_v": cache_gqa_v.reshape(dec_batch, depth, past, -1),
        "mla_ckv": cache_mla_ckv, "mla_kpe": cache_kpe,
    }
    tables = (_rope_tables(dec_seq, HEAD_DIM), _rope_tables(dec_seq, DIFF_QK))

    h = x_prompt.reshape(batch * seq, d)
    state, kpes = None, []
    for l in range(depth):
        h, (state, kpe_l) = _layer(h, l, weights, mod4, (0, batch * seq), batch, seq, None, None, True, state)
        kpes.append(kpe_l.reshape(batch, seq, MLA_ROPE))
    y_prompt = _final_norm(h, final_norm).reshape(batch, seq, d)

    h = x_sample.reshape(dec_batch * dec_seq, d)
    for l in range(depth):
        h, _ = _layer(h, l, weights, mod4, (1, dec_seq), dec_batch, dec_seq, caches, tables, False)
    y_sample = _final_norm(h, final_norm).reshape(dec_batch, dec_seq, d)

    dk, dv, gk, gv, ckv = state
    lead = (batch, depth, seq)
    return (y_prompt, y_sample,
            dk.reshape(lead + (diff_heads, 2, DIFF_QK)), dv.reshape(lead + (diff_heads, DIFF_V)),
            gk.reshape(lead + (kv_heads, HEAD_DIM)), gv.reshape(lead + (kv_heads, HEAD_DIM)),
            ckv, jnp.stack(kpes, axis=1))
```

```python
import functools
import math
import operator

import jax
import jax.numpy as jnp
from jax import lax
from jax.experimental import pallas as pl
from jax.experimental.pallas import tpu as pltpu

F32 = jnp.float32
BF16 = jnp.bfloat16

NORM_EPS = 1e-6
ROPE_THETA = 10000.0
GRID_W = 64
N_MOD = 9
DIFF_QK = 64
DIFF_V = 128
HEAD_DIM = 128
MLA_NOPE = 128
MLA_ROPE = 64
MLA_V = 128
LANES = 128
MOD_ROWS = 16
VMEM_LIMIT_BYTES = 56 * 1024 * 1024
W_TILE_BYTES_MAX = 6 * 1024 * 1024
LOG2_E = 1.4426950408889634


def _params(sem):
    return pltpu.CompilerParams(dimension_semantics=sem, vmem_limit_bytes=VMEM_LIMIT_BYTES)


def _silu(x):
    return x / (1.0 + jnp.exp(-x))


def _rms(x, gain):
    y = x * lax.rsqrt(jnp.mean(x * x, axis=-1, keepdims=True) + NORM_EPS)
    return y * gain


def _rope(x, cos, sin_lo, sin_hi, shift):
    return (x * cos + pltpu.roll(x, LANES - shift, 1) * sin_lo + pltpu.roll(x, shift, 1) * sin_hi)


_NT = (((1,), (1,)), ((), ()))


def _dot_nt(a, b):
    return lax.dot_general(a, b, _NT, preferred_element_type=F32)


def _sum(xs):
    return functools.reduce(operator.add, xs)


def _mod_kernel(c_ref, w_ref, b_ref, o_ref):
    a = _silu(c_ref[...]).astype(BF16)
    o_ref[...] = jnp.dot(a, w_ref[...].astype(BF16), preferred_element_type=F32) + b_ref[...]


def _modulation(cond, w_mod, b_mod):
    depth, d, n = w_mod.shape
    tn = 512 if n % 512 == 0 else n
    return pl.pallas_call(
        _mod_kernel,
        grid=(depth, n // tn),
        in_specs=[
            pl.BlockSpec((MOD_ROWS, d), lambda l, j: (0, 0)),
            pl.BlockSpec((None, d, tn), lambda l, j: (l, 0, j)),
            pl.BlockSpec((None, 1, tn), lambda l, j: (l, 0, j)),
        ],
        out_specs=pl.BlockSpec((None, MOD_ROWS, tn), lambda l, j: (l, 0, j)),
        out_shape=jax.ShapeDtypeStruct((depth, MOD_ROWS, n), F32),
        compiler_params=_params(("arbitrary", "arbitrary")),
        name="modulation",
    )(cond, w_mod, b_mod.reshape(depth, 1, n))


NORM_ROWS = 16


def _norm_rows(x_ref, o_ref, gain, shift):
    def body(r, carry):
        rows = pl.ds(pl.multiple_of(r * NORM_ROWS, NORM_ROWS), NORM_ROWS)
        x = x_ref[rows, :]
        y = x * lax.rsqrt(jnp.mean(x * x, axis=-1, keepdims=True) + NORM_EPS) * gain
        o_ref[rows, :] = (y if shift is None else y + shift).astype(o_ref.dtype)
        return carry

    lax.fori_loop(0, x_ref.shape[0] // NORM_ROWS, body, 0, unroll=8)


def _normmod_kernel(x_ref, g_ref, sh_ref, sc_ref, o_ref):
    _norm_rows(x_ref, o_ref, g_ref[...] * (1.0 + sc_ref[...]), sh_ref[...])


def _norm_kernel(x_ref, g_ref, o_ref):
    _norm_rows(x_ref, o_ref, g_ref[...], None)


def _row_tile(m, target):
    t = min(m, target)
    assert m % t == 0
    return t


def _normmod(x, gain3, l, mod4, grp, chunk_shift, chunk_scale, rows=None):
    m, d = x.shape
    m = m if rows is None else rows
    g_base, g_rows = grp
    tr = _row_tile(min(g_rows, m), 512)
    return pl.pallas_call(
        _normmod_kernel,
        grid=(m // tr,),
        in_specs=[
            pl.BlockSpec((tr, d), lambda i: (i, 0)),
            pl.BlockSpec((None, 1, d), lambda i: (l, 0, 0)),
            pl.BlockSpec((None, None, 1, d), lambda i: (l, g_base + (i * tr) // g_rows, 0, chunk_shift)),
            pl.BlockSpec((None, None, 1, d), lambda i: (l, g_base + (i * tr) // g_rows, 0, chunk_scale)),
        ],
        out_specs=pl.BlockSpec((tr, d), lambda i: (i, 0)),
        out_shape=jax.ShapeDtypeStruct((m, d), BF16),
        compiler_params=_params(("arbitrary",)),
        name="normmod",
    )(x, gain3, mod4, mod4)


def _final_norm(x, gain):
    m, d = x.shape
    tr = _row_tile(m, 512)
    return pl.pallas_call(
        _norm_kernel,
        grid=(m // tr,),
        in_specs=[pl.BlockSpec((tr, d), lambda i: (i, 0)), pl.BlockSpec((1, d), lambda i: (0, 0))],
        out_specs=pl.BlockSpec((tr, d), lambda i: (i, 0)),
        out_shape=jax.ShapeDtypeStruct((m, d), F32),
        compiler_params=_params(("arbitrary",)),
        name="final_norm",
    )(x, gain.reshape(1, d))


def _mm_kernel(a_ref, w_ref, o_ref):
    a = a_ref[...].astype(BF16)
    o_ref[...] = jnp.dot(a, w_ref[...].astype(BF16), preferred_element_type=F32).astype(o_ref.dtype)


def _mm_t_kernel(a_ref, wt_ref, o_ref):
    o_ref[...] = _dot_nt(a_ref[...], wt_ref[...].astype(BF16)).astype(o_ref.dtype)


def _mm_t_tail_kernel(a_ref, wt_ref, o_ref, *, valid):
    wt = wt_ref[...]
    row = lax.broadcasted_iota(jnp.int32, wt.shape, 0)
    wt = jnp.where(row < valid, wt, 0.0).astype(BF16)
    o_ref[...] = _dot_nt(a_ref[...], wt)


def _swiglu_and_norm_next(a_ref, nxt_ref, rows, xc_ref, gain, shift, wg_ref, wu_ref, o_ref):
    x = xc_ref[...]
    y = x * lax.rsqrt(jnp.mean(x * x, axis=-1, keepdims=True) + NORM_EPS) * gain + shift
    nxt_ref[rows, :] = y.astype(BF16)
    a = a_ref[...]
    g = jnp.dot(a, wg_ref[...].astype(BF16), preferred_element_type=F32)
    u = jnp.dot(a, wu_ref[...].astype(BF16), preferred_element_type=F32)
    o_ref[...] = (_silu(g) * u).astype(o_ref.dtype)


def _mm_swiglu_fused_kernel(a0_ref, xc_ref, g_ref, sh_ref, sc_ref, wg_ref, wu_ref, o_ref, buf0, buf1, *,
                            n_chunks, chunk):
    i, j = pl.program_id(0), pl.program_id(1)

    @pl.when((i == 0) & (j == 0))
    def _():
        buf0[...] = a0_ref[...]

    gain = g_ref[...] * (1.0 + sc_ref[...])
    rows = pl.ds(pl.multiple_of(jnp.minimum(j, n_chunks - 1) * chunk, chunk), chunk)
    args = (rows, xc_ref, gain, sh_ref[...], wg_ref, wu_ref, o_ref)

    @pl.when(i % 2 == 0)
    def _():
        _swiglu_and_norm_next(buf0, buf1, *args)

    @pl.when(i % 2 == 1)
    def _():
        _swiglu_and_norm_next(buf1, buf0, *args)


def _mm_resid_kernel(*refs, widths, coef, nk, tk):
    n_a = len(widths)
    a_refs = refs[:n_a]
    w_ref, x_ref, gate_ref, o_ref, acc_ref = refs[n_a:]
    if nk == 1:
        row = 0
        parts = []
        for a_ref, width in zip(a_refs, widths):
            parts.append(jnp.dot(a_ref[...], w_ref[row:row + width, :].astype(BF16), preferred_element_type=F32))
            row += width
        o_ref[...] = x_ref[...] + (coef * gate_ref[...]) * _sum(parts)
        return
    k = pl.program_id(2)
    a_ref, = a_refs
    depth_k = a_ref.shape[1]
    for kk in range(nk):
        @pl.when(k == kk)
        def _(kk=kk):
            rows = min(tk, depth_k - kk * tk)
            part = jnp.dot(a_ref[:, kk * tk:kk * tk + rows], w_ref[0:rows, :].astype(BF16),
                           preferred_element_type=F32)
            if kk > 0:
                part = acc_ref[...] + part
            if kk == nk - 1:
                o_ref[...] = x_ref[...] + (coef * gate_ref[...]) * part
            else:
                acc_ref[...] = part


def _col_tile(n, target=256):
    return target if n % target == 0 else n


def _matmul_t_tail(a, wt, l, row0, name):
    m, k = a.shape
    valid = wt.shape[1] - row0
    assert row0 % LANES == 0 and 0 < valid <= LANES
    tm = _row_tile(m, 1024)
    return pl.pallas_call(
        functools.partial(_mm_t_tail_kernel, valid=valid),
        grid=(m // tm,),
        in_specs=[
            pl.BlockSpec((tm, k), lambda i: (i, 0)),
            pl.BlockSpec((None, LANES, k), lambda i: (l, row0 // LANES, 0)),
        ],
        out_specs=pl.BlockSpec((tm, LANES), lambda i: (i, 0)),
        out_shape=jax.ShapeDtypeStruct((m, LANES), F32),
        compiler_params=_params(("arbitrary",)),
        name=name,
    )(a, wt)


def _matmul_t(a, wt, l, out_dtype, name, n, tn_target=256):
    m, k = a.shape
    tm = _row_tile(m, 1024)
    tn = _col_tile(n, tn_target)
    return pl.pallas_call(
        _mm_t_kernel,
        grid=(m // tm, n // tn),
        in_specs=[
            pl.BlockSpec((tm, k), lambda i, j: (i, 0)),
            pl.BlockSpec((None, tn, k), lambda i, j: (l, j, 0)),
        ],
        out_specs=pl.BlockSpec((tm, tn), lambda i, j: (i, j)),
        out_shape=jax.ShapeDtypeStruct((m, n), out_dtype),
        compiler_params=_params(("arbitrary", "arbitrary")),
        name=name,
    )(a, wt)


def _matmul(a, w, l, out_dtype, name, n=None, tn_target=256):
    m, k = a.shape
    n = w.shape[-1] if n is None else n
    tm = _row_tile(m, 1024)
    tn = _col_tile(n, tn_target)
    return pl.pallas_call(
        _mm_kernel,
        grid=(m // tm, n // tn),
        in_specs=[
            pl.BlockSpec((tm, k), lambda i, j: (i, 0)),
            pl.BlockSpec((None, k, tn), lambda i, j: (l, 0, j)),
        ],
        out_specs=pl.BlockSpec((tm, tn), lambda i, j: (i, j)),
        out_shape=jax.ShapeDtypeStruct((m, n), out_dtype),
        compiler_params=_params(("arbitrary", "arbitrary")),
        name=name,
    )(a, w)


def _matmul_cache(a4, l, w, lw, out_dtype, name):
    b, _, s, k = a4.shape
    n = w.shape[-1]
    tn = n
    return pl.pallas_call(
        _mm_kernel,
        grid=(b, n // tn),
        in_specs=[
            pl.BlockSpec((None, None, s, k), lambda i, j: (i, l, 0, 0)),
            pl.BlockSpec((None, k, tn), lambda i, j: (lw, 0, j)),
        ],
        out_specs=pl.BlockSpec((s, tn), lambda i, j: (i, j)),
        out_shape=jax.ShapeDtypeStruct((b * s, n), out_dtype),
        compiler_params=_params(("arbitrary", "arbitrary")),
        name=name,
    )(a4, w)


def _norm_swiglu(x, gain3, l, mod4, grp, chunk_shift, chunk_scale, w):
    m, k = x.shape
    f = w.shape[-1] // 2
    g_base, g_rows = grp
    tm = _row_tile(g_rows, 1024)
    tn = _col_tile(f)
    n_i, nj = m // tm, f // tn
    chunk = 16
    while chunk * nj < tm or tm % chunk:
        chunk += 16
    n_chunks = tm // chunk
    a0 = _normmod(x, gain3, l, mod4, grp, chunk_shift, chunk_scale, rows=tm)
    nxt = lambda i: jnp.minimum(i + 1, n_i - 1)
    mod_spec = lambda c: pl.BlockSpec((None, None, 1, k), lambda i, j: (l, g_base + (nxt(i) * tm) // g_rows, 0, c))
    kern = functools.partial(_mm_swiglu_fused_kernel, n_chunks=n_chunks, chunk=chunk)
    return pl.pallas_call(
        kern,
        grid=(n_i, nj),
        in_specs=[
            pl.BlockSpec((tm, k), lambda i, j: (0, 0), pipeline_mode=pl.Buffered(1)),
            pl.BlockSpec((chunk, k), lambda i, j: (nxt(i) * n_chunks + jnp.minimum(j, n_chunks - 1), 0)),
            pl.BlockSpec((None, 1, k), lambda i, j: (l, 0, 0)),
            mod_spec(chunk_shift),
            mod_spec(chunk_scale),
            pl.BlockSpec((None, k, tn), lambda i, j: (l, 0, j)),
            pl.BlockSpec((None, k, tn), lambda i, j: (l, 0, j + nj)),
        ],
        out_specs=pl.BlockSpec((tm, tn), lambda i, j: (i, j)),
        out_shape=jax.ShapeDtypeStruct((m, f), BF16),
        scratch_shapes=[pltpu.VMEM((tm, k), BF16)] * 2,
        compiler_params=_params(("arbitrary", "arbitrary")),
        name="ffn_in",
    )(a0, x, gain3, mod4, mod4, w, w)


def _matmul_resid(a_parts, w, l, x, mod4, grp, chunk, coef, name, tn_target=256):
    m = a_parts[0].shape[0]
    widths = tuple(a.shape[1] for a in a_parts)
    k = sum(widths)
    n = w.shape[-1]
    g_base, g_rows = grp
    tm = _row_tile(g_rows, 1024)
    tn = _col_tile(n, tn_target)
    nk, tk = 1, k
    while len(a_parts) == 1 and tk * tn * 4 > W_TILE_BYTES_MAX:
        nk += 1
        tk = pl.cdiv(pl.cdiv(k, nk), LANES) * LANES
    assert (nk - 1) * tk < k
    per_chunk = n // tn
    kern = functools.partial(_mm_resid_kernel, widths=widths, coef=coef, nk=nk, tk=tk)
    a_mode = {"pipeline_mode": pl.Buffered(1)} if nk > 1 else {}
    return pl.pallas_call(
        kern,
        grid=(m // tm, n // tn, nk),
        in_specs=[pl.BlockSpec((tm, width), lambda i, j, kk: (i, 0), **a_mode) for width in widths] + [
            pl.BlockSpec((None, tk, tn), lambda i, j, kk: (l, kk, j)),
            pl.BlockSpec((tm, tn), lambda i, j, kk: (i, j)),
            pl.BlockSpec((None, None, 1, tn), lambda i, j, kk: (l, g_base + (i * tm) // g_rows, 0, chunk * per_chunk + j)),
        ],
        out_specs=pl.BlockSpec((tm, tn), lambda i, j, kk: (i, j)),
        out_shape=jax.ShapeDtypeStruct((m, n), F32),
        scratch_shapes=[pltpu.VMEM((tm, tn), F32)],
        compiler_params=_params(("arbitrary", "arbitrary", "arbitrary")),
        name=name,
    )(*a_parts, w, x, mod4)


def _post_kernel(*refs, dims, rope, emit_state, n_alias):
    n_dqk, n_dv, n_gq, n_gk, q_lora, kv_lora = dims
    p_ref, pk_ref, gqn_ref, gkn_ref, mqn_ref, mkvn_ref = refs[:6]
    pos = 6
    if rope:
        c128, lo128, hi128, c64, lo64, hi64 = (r[...] for r in refs[pos:pos + 6])
        pos += 6
    pos += n_alias
    qkv_ref, cq_ref, ckv_ref, kpe_ref = refs[pos:pos + 4]
    pos += 4
    if emit_state:
        dk_f_ref, dv_f_ref, gk_f_ref, gv_f_ref, ckv_f_ref = refs[pos:pos + 5]

    col = 0
    for _ in range(n_dqk // LANES):
        x = p_ref[:, col:col + LANES]
        if rope:
            x = _rope(x, c64, lo64, hi64, DIFF_QK // 4)
        qkv_ref[:, col:col + LANES] = x.astype(BF16)
        col += LANES
    if emit_state:
        dk_f_ref[...] = p_ref[:, n_dqk // 2:n_dqk]
    dv = p_ref[:, col:col + n_dv]
    qkv_ref[:, col:col + n_dv] = dv.astype(BF16)
    if emit_state:
        dv_f_ref[...] = dv
    col += n_dv
    for _ in range(n_gq // LANES):
        x = _rms(p_ref[:, col:col + LANES], gqn_ref[...])
        if rope:
            x = _rope(x, c128, lo128, hi128, HEAD_DIM // 4)
        qkv_ref[:, col:col + LANES] = x.astype(BF16)
        col += LANES
    for h in range(n_gk // LANES):
        x = _rms(p_ref[:, col:col + LANES], gkn_ref[...])
        if emit_state:
            gk_f_ref[:, h * LANES:(h + 1) * LANES] = x
        if rope:
            x = _rope(x, c128, lo128, hi128, HEAD_DIM // 4)
        qkv_ref[:, col:col + LANES] = x.astype(BF16)
        col += LANES
    gv = p_ref[:, col:col + n_gk]
    qkv_ref[:, col:col + n_gk] = gv.astype(BF16)
    if emit_state:
        gv_f_ref[...] = gv
    col += n_gk
    cq_ref[...] = _rms(p_ref[:, col:col + q_lora], mqn_ref[...]).astype(BF16)
    col += q_lora
    ckv = _rms(p_ref[:, col:col + kv_lora], mkvn_ref[...])
    ckv_ref[...] = ckv.astype(BF16)
    if emit_state:
        ckv_f_ref[...] = ckv
    kpe = pk_ref[...]
    if rope:
        kpe = _rope(kpe, c64, lo64, hi64, MLA_ROPE // 4)
    kpe_ref[...] = kpe.astype(BF16)


def _post(p, pk, l, depth, dims, gqn, gkn, mqn, mkvn, tables, t_len, emit_state, state_in):
    m, n = p.shape
    n_dqk, n_dv, n_gq, n_gk, q_lora, kv_lora = dims
    n_qkv = n_dqk + n_dv + n_gq + 2 * n_gk
    rope = tables is not None
    tr = _row_tile(t_len, 256)
    nt = t_len // tr
    row_spec = lambda width: pl.BlockSpec((tr, width), lambda i: (i, 0))
    gain_spec = lambda width: pl.BlockSpec((None, 1, width), lambda i: (l, 0, 0))
    in_specs = [row_spec(n), row_spec(LANES), gain_spec(HEAD_DIM), gain_spec(HEAD_DIM), gain_spec(q_lora),
                gain_spec(kv_lora)]
    args = [p, pk, gqn, gkn, mqn, mkvn]
    if rope:
        in_specs += [pl.BlockSpec((tr, LANES), lambda i: (i % nt, 0))] * 6
        args += list(tables)
    out_widths = [(n_qkv, BF16), (q_lora, BF16), (kv_lora, BF16), (LANES, BF16)]
    out_specs = [row_spec(width) for width, _ in out_widths]
    out_shape = [jax.ShapeDtypeStruct((m, width), dt) for width, dt in out_widths]
    aliases = {}
    n_alias = 0
    if emit_state:
        state_widths = [n_dqk // 2, n_dv, n_gk, n_gk, kv_lora]
        if state_in is not None:
            n_alias = len(state_in)
            aliases = {len(args) + s: len(out_shape) + s for s in range(n_alias)}
            in_specs += [pl.BlockSpec(memory_space=pl.ANY)] * n_alias
            args += list(state_in)
        out_specs += [pl.BlockSpec((None, None, tr, width), lambda i: (i // nt, l, i % nt, 0)) for width in state_widths]
        out_shape += [jax.ShapeDtypeStruct((m // t_len, depth, t_len, width), F32) for width in state_widths]
    kern = functools.partial(_post_kernel, dims=dims, rope=rope, emit_state=emit_state, n_alias=n_alias)
    return pl.pallas_call(
        kern,
        grid=(m // tr,),
        in_specs=in_specs,
        out_specs=out_specs,
        out_shape=out_shape,
        input_output_aliases=aliases,
        compiler_params=_params(("arbitrary",)),
        name="mixer_post",
    )(*args)


def _attend(q_parts, segments, c):
    s = [_sum([_dot_nt(q, k) for q, k in zip(q_parts, ks)]) for ks, _ in segments]
    m = functools.reduce(jnp.maximum, [jnp.max(x, axis=-1, keepdims=True) for x in s])
    e = [jnp.exp2((x - m) * c) for x in s]
    den = _sum([jnp.sum(x, axis=-1, keepdims=True) for x in e])
    o = _sum([jnp.dot(x.astype(BF16), v, preferred_element_type=F32) for x, (_, v) in zip(e, segments)])
    return o / den


def _diff_attn_kernel(*refs, heads, has_cache, lam_init):
    q_ref, k_ref, v_ref = refs[:3]
    pos = 3
    if has_cache:
        kc_ref, vc_ref = refs[pos:pos + 2]
        pos += 2
    lq1, lk1, lq2, lk2, sub_ref, o_ref = refs[pos:pos + 6]
    lam = (jnp.exp(jnp.sum(lq1[...] * lk1[...], axis=-1, keepdims=True))
           - jnp.exp(jnp.sum(lq2[...] * lk2[...], axis=-1, keepdims=True)) + lam_init)
    c = DIFF_QK ** -0.5 * LOG2_E
    lane = lax.broadcasted_iota(jnp.int32, (q_ref.shape[0], LANES), 1)
    for h in range(heads):
        cols = slice(h * LANES, (h + 1) * LANES)
        q = q_ref[:, cols]
        zero = jnp.zeros_like(q)
        segments = [([k_ref[:, cols]], v_ref[:, cols])]
        if has_cache:
            segments.insert(0, ([kc_ref[:, cols].astype(BF16)], vc_ref[:, cols].astype(BF16)))
        o0 = _attend([jnp.where(lane < DIFF_QK, q, zero)], segments, c)
        o1 = _attend([jnp.where(lane >= DIFF_QK, q, zero)], segments, c)
        o = o0 - lam * o1
        o_ref[:, cols] = (_rms(o, sub_ref[...]) * (1.0 - lam_init)).astype(o_ref.dtype)


def _gqa_attn_kernel(*refs, kv_heads, group, has_cache):
    q_ref, k_ref, v_ref = refs[:3]
    pos = 3
    if has_cache:
        kc_ref, vc_ref = refs[pos:pos + 2]
        pos += 2
    o_ref = refs[pos]
    c = HEAD_DIM ** -0.5 * LOG2_E
    for kvh in range(kv_heads):
        cols = slice(kvh * LANES, (kvh + 1) * LANES)
        segments = [([k_ref[:, cols]], v_ref[:, cols])]
        if has_cache:
            segments.insert(0, ([kc_ref[:, cols].astype(BF16)], vc_ref[:, cols].astype(BF16)))
        for g in range(group):
            qcols = slice((kvh * group + g) * LANES, (kvh * group + g + 1) * LANES)
            o_ref[:, qcols] = _attend([q_ref[:, qcols]], segments, c).astype(o_ref.dtype)


def _mla_attn_kernel(*refs, heads, has_cache, rope):
    mq_ref, kv_ref, kp_ref = refs[:3]
    pos = 3
    if has_cache:
        kvc_ref, kpc_ref = refs[pos:pos + 2]
        pos += 2
    if rope:
        c64, lo64, hi64 = (r[...] for r in refs[pos:pos + 3])
        pos += 3
    o_ref = refs[pos]
    c = (MLA_NOPE + MLA_ROPE) ** -0.5 * LOG2_E
    kp = kp_ref[...]
    if has_cache:
        kpc = kpc_ref[...].astype(BF16)
    for h in range(heads):
        cols = slice(h * LANES, (h + 1) * LANES)
        kcols = slice(2 * h * LANES, (2 * h + 1) * LANES)
        vcols = slice((2 * h + 1) * LANES, (2 * h + 2) * LANES)
        qn = mq_ref[:, cols].astype(BF16)
        qp = mq_ref[:, (heads + h) * LANES:(heads + h + 1) * LANES]
        if rope:
            qp = _rope(qp, c64, lo64, hi64, MLA_ROPE // 4)
        q = jnp.concatenate([qn, qp.astype(BF16)], axis=-1)
        segments = [([jnp.concatenate([kv_ref[:, kcols], kp], axis=-1)], kv_ref[:, vcols])]
        if has_cache:
            segments.insert(0, ([jnp.concatenate([kvc_ref[:, kcols], kpc], axis=-1)], kvc_ref[:, vcols]))
        o_ref[:, cols] = _attend([q], segments, c).astype(o_ref.dtype)


def _q_tile(t):
    return 256 if t % 256 == 0 else t


def _attn_call(kern, name, batch, t, out_width, q_specs, kv_specs, cache_specs, extra_specs, args):
    tq = _q_tile(t)
    nq = t // tq
    in_specs = ([pl.BlockSpec((tq, w), lambda b, i, c=c: (b * nq + i, c)) for w, c in q_specs]
                + [pl.BlockSpec((t, w), lambda b, i, c=c: (b, c)) for w, c in kv_specs]
                + cache_specs + extra_specs)
    return pl.pallas_call(
        kern,
        grid=(batch, nq),
        in_specs=in_specs,
        out_specs=pl.BlockSpec((tq, out_width), lambda b, i: (b * nq + i, 0)),
        out_shape=jax.ShapeDtypeStruct((batch * t, out_width), BF16),
        compiler_params=_params(("arbitrary", "arbitrary")),
        name=name,
    )(*args)


def _cache_spec(l, s_cache, width, col):
    return pl.BlockSpec((None, None, s_cache, width), lambda b, i: (b, l, 0, col))


def _diff_attention(qkv, l, batch, t, heads, cache_k4, cache_v4, lparams, subln):
    has_cache = cache_k4 is not None
    w = heads * LANES
    cache_specs, args = [], [qkv, qkv, qkv]
    if has_cache:
        cache_specs = [_cache_spec(l, cache_k4.shape[2], w, 0)] * 2
        args += [cache_k4, cache_v4]
    extra = [pl.BlockSpec((None, 1, DIFF_QK), lambda b, i: (l, 0, 0))] * 4
    extra += [pl.BlockSpec((None, 1, DIFF_V), lambda b, i: (l, 0, 0))]
    args += list(lparams) + [subln]
    lam_init = 0.8 - 0.6 * math.exp(-0.3 * l)
    kern = functools.partial(_diff_attn_kernel, heads=heads, has_cache=has_cache, lam_init=lam_init)
    return _attn_call(kern, "diff_attention", batch, t, w, [(w, 0)], [(w, 1), (w, 2)], cache_specs, extra, args)


def _gqa_attention(qkv, l, batch, t, col0, q_heads, kv_heads, cache_k4, cache_v4):
    has_cache = cache_k4 is not None
    qw, kw = q_heads * HEAD_DIM, kv_heads * HEAD_DIM
    assert col0 % qw == 0 and (col0 + qw) % kw == 0
    k_blk = (col0 + qw) // kw
    cache_specs, args = [], [qkv, qkv, qkv]
    if has_cache:
        cache_specs = [_cache_spec(l, cache_k4.shape[2], kw, 0)] * 2
        args += [cache_k4, cache_v4]
    kern = functools.partial(_gqa_attn_kernel, kv_heads=kv_heads, group=q_heads // kv_heads, has_cache=has_cache)
    return _attn_call(kern, "gqa_attention", batch, t, qw, [(qw, col0 // qw)], [(kw, k_blk), (kw, k_blk + 1)],
                      cache_specs, [], args)


def _mla_attention(mq, kv, kpe, l, batch, t, heads, kv_cache, cache_kpe4, tables64):
    has_cache = kv_cache is not None
    rope = tables64 is not None
    tq = _q_tile(t)
    cache_specs, extra, args = [], [], [mq, kv, kpe]
    if has_cache:
        s_cache = cache_kpe4.shape[2]
        cache_specs = [pl.BlockSpec((s_cache, kv.shape[1]), lambda b, i: (b, 0)), _cache_spec(l, s_cache, LANES, 0)]
        args += [kv_cache, cache_kpe4]
    if rope:
        extra = [pl.BlockSpec((tq, LANES), lambda b, i: (i, 0))] * 3
        args += list(tables64)
    kern = functools.partial(_mla_attn_kernel, heads=heads, has_cache=has_cache, rope=rope)
    return _attn_call(kern, "mla_attention", batch, t, heads * MLA_V, [(mq.shape[1], 0)],
                      [(kv.shape[1], 0), (LANES, 0)], cache_specs, extra, args)


def _rope_tables(t_len, width):
    half = width // 2
    quarter = half // 2
    t = jnp.arange(t_len, dtype=jnp.int32)
    rows = (t // GRID_W).astype(F32)
    cols = (t % GRID_W).astype(F32)
    lane = jnp.arange(LANES, dtype=jnp.int32) % width
    inv = ROPE_THETA ** (-(jnp.arange(0, half, 2, dtype=F32) / half))
    inv_lane = inv[(lane % half) % quarter]
    pos = jnp.where((lane < half)[None, :], rows[:, None], cols[:, None])
    ang = pos * inv_lane[None, :]
    cos, sin = jnp.cos(ang), jnp.sin(ang)
    first = ((lane % half) < quarter)[None, :]
    zero = jnp.zeros_like(sin)
    return cos, jnp.where(first, -sin, zero), jnp.where(first, zero, sin)


def _layer(x, l, w, mod4, grp, batch, t, caches, tables, emit_state, state_in=None):
    ffn = lambda h, w_in, w_out, norm, c0, name: _matmul_resid(
        [_norm_swiglu(h, norm, l, mod4, grp, c0, c0 + 1, w_in)],
        w_out, l, h, mod4, grp, c0 + 2, 0.5, name, tn_target=512)
    x = ffn(x, w["ffn1_w_in"], w["ffn1_w_out"], w["norm_ffn1"], 0, "ffn1_out")

    u = _normmod(x, w["norm_mix"], l, mod4, grp, 3, 4)
    dims = w["dims"]
    n_dqk, n_dv, n_gq, n_gk, q_lora, kv_lora = dims
    n_main = n_dqk + n_dv + n_gq + 2 * n_gk + q_lora + kv_lora
    p = _matmul_t(u, w["w_in_t"], l, F32, "mixer_in", n_main, tn_target=512)
    pk = _matmul_t_tail(u, w["w_in_t"], l, n_main, "mixer_in_kpe")
    post = _post(p, pk, l, w["ffn1_w_in"].shape[0], dims, w["gqa_q_norm"], w["gqa_k_norm"], w["mla_q_norm"],
                 w["mla_kv_norm"], tables[0] + tables[1] if tables is not None else None, t, emit_state, state_in)
    qkv, cqn, ckvn, kpe = post[:4]

    diff_heads = n_dv // DIFF_V
    q_heads, kv_heads = n_gq // HEAD_DIM, n_gk // HEAD_DIM
    mla_heads = w["mla_w_ukv"].shape[-1] // (MLA_NOPE + MLA_V)
    has_cache = caches is not None
    o_diff = _diff_attention(qkv, l, batch, t, diff_heads,
                             caches["diff_k"] if has_cache else None, caches["diff_v"] if has_cache else None,
                             w["diff_l"], w["diff_subln"])
    o_gqa = _gqa_attention(qkv, l, batch, t, n_dqk + n_dv, q_heads, kv_heads,
                           caches["gqa_k"] if has_cache else None, caches["gqa_v"] if has_cache else None)
    mq = _matmul(cqn, w["mla_w_uq"], l, F32, "mla_q_up", tn_target=1024)
    kv = _matmul(ckvn, w["mla_w_ukv"], l, BF16, "mla_kv_up", tn_target=w["mla_w_ukv"].shape[-1])
    kv_cache = _matmul_cache(caches["mla_ckv"], l, w["mla_w_ukv"], l, BF16, "mla_kv_up_cache") if has_cache else None
    o_mla = _mla_attention(mq, kv, kpe, l, batch, t, mla_heads, kv_cache,
                           caches["mla_kpe"] if has_cache else None, tables[1] if tables is not None else None)
    x = _matmul_resid([o_diff, o_gqa, o_mla], w["w_out"], l, x, mod4, grp, 5, 1.0, "mixer_out", tn_target=512)

    x = ffn(x, w["ffn2_w_in"], w["ffn2_w_out"], w["norm_ffn2"], 6, "ffn2_out")
    state = (list(post[4:]), pk[:, :MLA_ROPE]) if emit_state else None
    return x, state


def kernel(x_prompt, x_sample, cache_diff_k, cache_diff_v, cache_gqa_k, cache_gqa_v, cache_mla_ckv, cache_mla_kpe, c, c_ctx, w_mod, b_mod, norm_ffn1, norm_mix, norm_ffn2, ffn1_w_in, ffn1_w_out, ffn2_w_in, ffn2_w_out, w_in, w_out, diff_lq1, diff_lk1, diff_lq2, diff_lk2, diff_subln, gqa_q_norm, gqa_k_norm, mla_q_norm, mla_kv_norm, mla_w_uq, mla_w_ukv, final_norm):
    batch, seq, d = x_prompt.shape
    dec_batch, dec_seq, _ = x_sample.shape
    depth = w_mod.shape[0]
    past = cache_diff_k.shape[2]
    diff_heads = cache_diff_k.shape[3]
    kv_heads = cache_gqa_k.shape[3]
    q_lora, kv_lora = mla_q_norm.shape[-1], mla_kv_norm.shape[-1]
    mla_heads = mla_w_ukv.shape[-1] // (MLA_NOPE + MLA_V)
    n_dqk = 2 * diff_heads * 2 * DIFF_QK
    n_dv = diff_heads * DIFF_V
    n_gk = kv_heads * HEAD_DIM
    n_gq = w_in.shape[-1] - (n_dqk + n_dv + 2 * n_gk + q_lora + kv_lora + MLA_ROPE)
    dims = (n_dqk, n_dv, n_gq, n_gk, q_lora, kv_lora)
    assert dec_batch + 1 <= MOD_ROWS

    cond = jnp.zeros((MOD_ROWS, d), F32).at[0].set(c_ctx).at[1:1 + dec_batch].set(c)
    mod4 = _modulation(cond, w_mod, b_mod).reshape(depth, MOD_ROWS, 1, N_MOD * d)

    uq = mla_w_uq.reshape(depth, q_lora, mla_heads, MLA_NOPE + MLA_ROPE)
    uq_pe = jnp.pad(uq[..., MLA_NOPE:], ((0, 0), (0, 0), (0, 0), (0, LANES - MLA_ROPE)))
    uq = jnp.concatenate([uq[..., :MLA_NOPE].reshape(depth, q_lora, -1), uq_pe.reshape(depth, q_lora, -1)], axis=-1)
    cache_kpe = jnp.pad(cache_mla_kpe, ((0, 0), (0, 0), (0, 0), (0, LANES - MLA_ROPE)))
    w_in_t = jnp.swapaxes(w_in, 1, 2)

    r3 = lambda a: a.reshape(depth, 1, a.shape[-1])
    weights = {
        "ffn1_w_in": ffn1_w_in, "ffn1_w_out": ffn1_w_out, "ffn2_w_in": ffn2_w_in, "ffn2_w_out": ffn2_w_out,
        "w_in_t": w_in_t, "w_out": w_out, "mla_w_uq": uq, "mla_w_ukv": mla_w_ukv,
        "norm_ffn1": r3(norm_ffn1), "norm_mix": r3(norm_mix), "norm_ffn2": r3(norm_ffn2),
        "gqa_q_norm": r3(gqa_q_norm), "gqa_k_norm": r3(gqa_k_norm),
        "mla_q_norm": r3(mla_q_norm), "mla_kv_norm": r3(mla_kv_norm),
        "diff_l": (r3(diff_lq1), r3(diff_lk1), r3(diff_lq2), r3(diff_lk2)), "diff_subln": r3(diff_subln),
        "dims": dims,
    }
    caches = {
        "diff_k": cache_diff_k.reshape(dec_batch, depth, past, -1),
        "diff_v": cache_diff_v.reshape(dec_batch, depth, past, -1),
        "gqa_k": cache_gqa_k.reshape(dec_batch, depth, past, -1),
        "gqa_v": cache_gqa_v.reshape(dec_batch, depth, past, -1),
        "mla_ckv": cache_mla_ckv, "mla_kpe": cache_kpe,
    }
    tables = (_rope_tables(dec_seq, HEAD_DIM), _rope_tables(dec_seq, DIFF_QK))

    h = x_prompt.reshape(batch * seq, d)
    state, kpes = None, []
    for l in range(depth):
        h, (state, kpe_l) = _layer(h, l, weights, mod4, (0, batch * seq), batch, seq, None, None, True, state)
        kpes.append(kpe_l.reshape(batch, seq, MLA_ROPE))
    y_prompt = _final_norm(h, final_norm).reshape(batch, seq, d)

    h = x_sample.reshape(dec_batch * dec_seq, d)
    for l in range(depth):
        h, _ = _layer(h, l, weights, mod4, (1, dec_seq), dec_batch, dec_seq, caches, tables, False)
    y_sample = _final_norm(h, final_norm).reshape(dec_batch, dec_seq, d)

    dk, dv, gk, gv, ckv = state
    lead = (batch, depth, seq)
    return (y_prompt, y_sample,
            dk.reshape(lead + (diff_heads, 2, DIFF_QK)), dv.reshape(lead + (diff_heads, DIFF_V)),
            gk.reshape(lead + (kv_heads, HEAD_DIM)), gv.reshape(lead + (kv_heads, HEAD_DIM)),
            ckv, jnp.stack(kpes, axis=1))
```

```python
import functools
import math
import operator

import jax
import jax.numpy as jnp
from jax import lax
from jax.experimental import pallas as pl
from jax.experimental.pallas import tpu as pltpu

F32 = jnp.float32
BF16 = jnp.bfloat16

NORM_EPS = 1e-6
ROPE_THETA = 10000.0
GRID_W = 64
N_MOD = 9
DIFF_QK = 64
DIFF_V = 128
HEAD_DIM = 128
MLA_NOPE = 128
MLA_ROPE = 64
MLA_V = 128
LANES = 128
MOD_ROWS = 16
VMEM_LIMIT_BYTES = 56 * 1024 * 1024
W_TILE_BYTES_MAX = 6 * 1024 * 1024
LOG2_E = 1.4426950408889634


def _params(sem):
    return pltpu.CompilerParams(dimension_semantics=sem, vmem_limit_bytes=VMEM_LIMIT_BYTES)


def _silu(x):
    return x / (1.0 + jnp.exp(-x))


def _rms(x, gain):
    y = x * lax.rsqrt(jnp.mean(x * x, axis=-1, keepdims=True) + NORM_EPS)
    return y * gain


def _rope(x, cos, sin_lo, sin_hi, shift):
    return (x * cos + pltpu.roll(x, LANES - shift, 1) * sin_lo + pltpu.roll(x, shift, 1) * sin_hi)


_NT = (((1,), (1,)), ((), ()))


def _dot_nt(a, b):
    return lax.dot_general(a, b, _NT, preferred_element_type=F32)


def _sum(xs):
    return functools.reduce(operator.add, xs)


def _mod_kernel(c_ref, w_ref, b_ref, o_ref):
    a = _silu(c_ref[...]).astype(BF16)
    o_ref[...] = jnp.dot(a, w_ref[...].astype(BF16), preferred_element_type=F32) + b_ref[...]


def _modulation(cond, w_mod, b_mod):
    depth, d, n = w_mod.shape
    tn = 512 if n % 512 == 0 else n
    return pl.pallas_call(
        _mod_kernel,
        grid=(depth, n // tn),
        in_specs=[
            pl.BlockSpec((MOD_ROWS, d), lambda l, j: (0, 0)),
            pl.BlockSpec((None, d, tn), lambda l, j: (l, 0, j)),
            pl.BlockSpec((None, 1, tn), lambda l, j: (l, 0, j)),
        ],
        out_specs=pl.BlockSpec((None, MOD_ROWS, tn), lambda l, j: (l, 0, j)),
        out_shape=jax.ShapeDtypeStruct((depth, MOD_ROWS, n), F32),
        compiler_params=_params(("arbitrary", "arbitrary")),
        name="modulation",
    )(cond, w_mod, b_mod.reshape(depth, 1, n))


NORM_ROWS = 16


def _norm_rows(x_ref, o_ref, gain, shift):
    def body(r, carry):
        rows = pl.ds(pl.multiple_of(r * NORM_ROWS, NORM_ROWS), NORM_ROWS)
        x = x_ref[rows, :]
        y = x * lax.rsqrt(jnp.mean(x * x, axis=-1, keepdims=True) + NORM_EPS) * gain
        o_ref[rows, :] = (y if shift is None else y + shift).astype(o_ref.dtype)
        return carry

    lax.fori_loop(0, x_ref.shape[0] // NORM_ROWS, body, 0, unroll=8)


def _normmod_kernel(x_ref, g_ref, sh_ref, sc_ref, o_ref):
    _norm_rows(x_ref, o_ref, g_ref[...] * (1.0 + sc_ref[...]), sh_ref[...])


def _norm_kernel(x_ref, g_ref, o_ref):
    _norm_rows(x_ref, o_ref, g_ref[...], None)


def _row_tile(m, target):
    t = min(m, target)
    assert m % t == 0
    return t


def _normmod(x, gain3, l, mod4, grp, chunk_shift, chunk_scale, rows=None):
    m, d = x.shape
    m = m if rows is None else rows
    g_base, g_rows = grp
    tr = _row_tile(min(g_rows, m), 512)
    return pl.pallas_call(
        _normmod_kernel,
        grid=(m // tr,),
        in_specs=[
            pl.BlockSpec((tr, d), lambda i: (i, 0)),
            pl.BlockSpec((None, 1, d), lambda i: (l, 0, 0)),
            pl.BlockSpec((None, None, 1, d), lambda i: (l, g_base + (i * tr) // g_rows, 0, chunk_shift)),
            pl.BlockSpec((None, None, 1, d), lambda i: (l, g_base + (i * tr) // g_rows, 0, chunk_scale)),
        ],
        out_specs=pl.BlockSpec((tr, d), lambda i: (i, 0)),
        out_shape=jax.ShapeDtypeStruct((m, d), BF16),
        compiler_params=_params(("arbitrary",)),
        name="normmod",
    )(x, gain3, mod4, mod4)


def _final_norm(x, gain):
    m, d = x.shape
    tr = _row_tile(m, 512)
    return pl.pallas_call(
        _norm_kernel,
        grid=(m // tr,),
        in_specs=[pl.BlockSpec((tr, d), lambda i: (i, 0)), pl.BlockSpec((1, d), lambda i: (0, 0))],
        out_specs=pl.BlockSpec((tr, d), lambda i: (i, 0)),
        out_shape=jax.ShapeDtypeStruct((m, d), F32),
        compiler_params=_params(("arbitrary",)),
        name="final_norm",
    )(x, gain.reshape(1, d))


def _mm_kernel(a_ref, w_ref, o_ref):
    a = a_ref[...].astype(BF16)
    o_ref[...] = jnp.dot(a, w_ref[...].astype(BF16), preferred_element_type=F32).astype(o_ref.dtype)


def _mm_t_kernel(a_ref, wt_ref, o_ref):
    o_ref[...] = _dot_nt(a_ref[...], wt_ref[...].astype(BF16)).astype(o_ref.dtype)


def _mm_t_tail_kernel(a_ref, wt_ref, o_ref, *, valid):
    wt = wt_ref[...]
    row = lax.broadcasted_iota(jnp.int32, wt.shape, 0)
    wt = jnp.where(row < valid, wt, 0.0).astype(BF16)
    o_ref[...] = _dot_nt(a_ref[...], wt)


def _swiglu_and_norm_next(a_ref, nxt_ref, rows, xc_ref, gain, shift, wg_ref, wu_ref, o_ref):
    x = xc_ref[...]
    y = x * lax.rsqrt(jnp.mean(x * x, axis=-1, keepdims=True) + NORM_EPS) * gain + shift
    nxt_ref[rows, :] = y.astype(BF16)
    a = a_ref[...]
    g = jnp.dot(a, wg_ref[...].astype(BF16), preferred_element_type=F32)
    u = jnp.dot(a, wu_ref[...].astype(BF16), preferred_element_type=F32)
    o_ref[...] = (_silu(g) * u).astype(o_ref.dtype)


def _mm_swiglu_fused_kernel(a0_ref, xc_ref, g_ref, sh_ref, sc_ref, wg_ref, wu_ref, o_ref, buf0, buf1, *,
                            n_chunks, chunk):
    i, j = pl.program_id(0), pl.program_id(1)

    @pl.when((i == 0) & (j == 0))
    def _():
        buf0[...] = a0_ref[...]

    gain = g_ref[...] * (1.0 + sc_ref[...])
    rows = pl.ds(pl.multiple_of(jnp.minimum(j, n_chunks - 1) * chunk, chunk), chunk)
    args = (rows, xc_ref, gain, sh_ref[...], wg_ref, wu_ref, o_ref)

    @pl.when(i % 2 == 0)
    def _():
        _swiglu_and_norm_next(buf0, buf1, *args)

    @pl.when(i % 2 == 1)
    def _():
        _swiglu_and_norm_next(buf1, buf0, *args)


def _mm_resid_kernel(*refs, widths, coef):
    n_a = len(widths)
    a_refs = refs[:n_a]
    w_ref, x_ref, gate_ref, o_ref = refs[n_a:]
    row = 0
    parts = []
    for a_ref, width in zip(a_refs, widths):
        parts.append(jnp.dot(a_ref[...], w_ref[row:row + width, :].astype(BF16), preferred_element_type=F32))
        row += width
    o_ref[...] = x_ref[...] + (coef * gate_ref[...]) * _sum(parts)


def _mm_resid_split_kernel(a_hbm, w_ref, x_ref, gate_ref, o_ref, acc_ref, head_buf, rest_buf, sems, *,
                           coef, nk, tk, tm, depth_k):
    i, j, k = pl.program_id(0), pl.program_id(1), pl.program_id(2)
    n_i, n_j = pl.num_programs(0), pl.num_programs(1)
    width = lambda kk: min(tk, depth_k - kk * tk)

    def chunk_copy(tile, kk):
        src = a_hbm.at[pl.ds(tile * tm, tm), pl.ds(kk * tk, width(kk))]
        dst = head_buf if kk == 0 else rest_buf.at[:, pl.ds((kk - 1) * tk, width(kk))]
        return pltpu.make_async_copy(src, dst, sems.at[kk])

    @pl.when((j == 0) & (k == 0))
    def _():
        @pl.when(i == 0)
        def _():
            chunk_copy(0, 0).start()

        for kk in range(1, nk):
            chunk_copy(i, kk).start()
        chunk_copy(i, 0).wait()

    for kk in range(1, nk):
        @pl.when((j == 0) & (k == kk))
        def _(kk=kk):
            chunk_copy(i, kk).wait()

    @pl.when((j == n_j - 1) & (k == 1) & (i + 1 < n_i))
    def _():
        chunk_copy(i + 1, 0).start()

    for kk in range(nk):
        @pl.when(k == kk)
        def _(kk=kk):
            rows = width(kk)
            a = head_buf[...] if kk == 0 else rest_buf[:, (kk - 1) * tk:(kk - 1) * tk + rows]
            part = jnp.dot(a, w_ref[0:rows, :].astype(BF16), preferred_element_type=F32)
            if kk > 0:
                part = acc_ref[...] + part
            if kk == nk - 1:
                o_ref[...] = x_ref[...] + (coef * gate_ref[...]) * part
            else:
                acc_ref[...] = part


def _col_tile(n, target=256):
    return target if n % target == 0 else n


def _matmul_t_tail(a, wt, l, row0, name):
    m, k = a.shape
    valid = wt.shape[1] - row0
    assert row0 % LANES == 0 and 0 < valid <= LANES
    tm = _row_tile(m, 1024)
    return pl.pallas_call(
        functools.partial(_mm_t_tail_kernel, valid=valid),
        grid=(m // tm,),
        in_specs=[
            pl.BlockSpec((tm, k), lambda i: (i, 0)),
            pl.BlockSpec((None, LANES, k), lambda i: (l, row0 // LANES, 0)),
        ],
        out_specs=pl.BlockSpec((tm, LANES), lambda i: (i, 0)),
        out_shape=jax.ShapeDtypeStruct((m, LANES), F32),
        compiler_params=_params(("arbitrary",)),
        name=name,
    )(a, wt)


def _matmul_t(a, wt, l, out_dtype, name, n, tn_target=256):
    m, k = a.shape
    tm = _row_tile(m, 1024)
    tn = _col_tile(n, tn_target)
    return pl.pallas_call(
        _mm_t_kernel,
        grid=(m // tm, n // tn),
        in_specs=[
            pl.BlockSpec((tm, k), lambda i, j: (i, 0)),
            pl.BlockSpec((None, tn, k), lambda i, j: (l, j, 0)),
        ],
        out_specs=pl.BlockSpec((tm, tn), lambda i, j: (i, j)),
        out_shape=jax.ShapeDtypeStruct((m, n), out_dtype),
        compiler_params=_params(("arbitrary", "arbitrary")),
        name=name,
    )(a, wt)


def _matmul(a, w, l, out_dtype, name, n=None, tn_target=256):
    m, k = a.shape
    n = w.shape[-1] if n is None else n
    tm = _row_tile(m, 1024)
    tn = _col_tile(n, tn_target)
    return pl.pallas_call(
        _mm_kernel,
        grid=(m // tm, n // tn),
        in_specs=[
            pl.BlockSpec((tm, k), lambda i, j: (i, 0)),
            pl.BlockSpec((None, k, tn), lambda i, j: (l, 0, j)),
        ],
        out_specs=pl.BlockSpec((tm, tn), lambda i, j: (i, j)),
        out_shape=jax.ShapeDtypeStruct((m, n), out_dtype),
        compiler_params=_params(("arbitrary", "arbitrary")),
        name=name,
    )(a, w)


def _matmul_cache(a4, l, w, lw, out_dtype, name):
    b, _, s, k = a4.shape
    n = w.shape[-1]
    tn = n
    return pl.pallas_call(
        _mm_kernel,
        grid=(b, n // tn),
        in_specs=[
            pl.BlockSpec((None, None, s, k), lambda i, j: (i, l, 0, 0)),
            pl.BlockSpec((None, k, tn), lambda i, j: (lw, 0, j)),
        ],
        out_specs=pl.BlockSpec((s, tn), lambda i, j: (i, j)),
        out_shape=jax.ShapeDtypeStruct((b * s, n), out_dtype),
        compiler_params=_params(("arbitrary", "arbitrary")),
        name=name,
    )(a4, w)


def _norm_swiglu(x, gain3, l, mod4, grp, chunk_shift, chunk_scale, w):
    m, k = x.shape
    f = w.shape[-1] // 2
    g_base, g_rows = grp
    tm = _row_tile(g_rows, 1024)
    tn = _col_tile(f)
    n_i, nj = m // tm, f // tn
    chunk = 16
    while chunk * nj < tm or tm % chunk:
        chunk += 16
    n_chunks = tm // chunk
    a0 = _normmod(x, gain3, l, mod4, grp, chunk_shift, chunk_scale, rows=tm)
    nxt = lambda i: jnp.minimum(i + 1, n_i - 1)
    mod_spec = lambda c: pl.BlockSpec((None, None, 1, k), lambda i, j: (l, g_base + (nxt(i) * tm) // g_rows, 0, c))
    kern = functools.partial(_mm_swiglu_fused_kernel, n_chunks=n_chunks, chunk=chunk)
    return pl.pallas_call(
        kern,
        grid=(n_i, nj),
        in_specs=[
            pl.BlockSpec((tm, k), lambda i, j: (0, 0), pipeline_mode=pl.Buffered(1)),
            pl.BlockSpec((chunk, k), lambda i, j: (nxt(i) * n_chunks + jnp.minimum(j, n_chunks - 1), 0)),
            pl.BlockSpec((None, 1, k), lambda i, j: (l, 0, 0)),
            mod_spec(chunk_shift),
            mod_spec(chunk_scale),
            pl.BlockSpec((None, k, tn), lambda i, j: (l, 0, j)),
            pl.BlockSpec((None, k, tn), lambda i, j: (l, 0, j + nj)),
        ],
        out_specs=pl.BlockSpec((tm, tn), lambda i, j: (i, j)),
        out_shape=jax.ShapeDtypeStruct((m, f), BF16),
        scratch_shapes=[pltpu.VMEM((tm, k), BF16)] * 2,
        compiler_params=_params(("arbitrary", "arbitrary")),
        name="ffn_in",
    )(a0, x, gain3, mod4, mod4, w, w)


def _matmul_resid(a_parts, w, l, x, mod4, grp, chunk, coef, name, tn_target=256):
    m = a_parts[0].shape[0]
    widths = tuple(a.shape[1] for a in a_parts)
    k = sum(widths)
    n = w.shape[-1]
    g_base, g_rows = grp
    tm = _row_tile(g_rows, 1024)
    tn = _col_tile(n, tn_target)
    nk, tk = 1, k
    while len(a_parts) == 1 and tk * tn * 4 > W_TILE_BYTES_MAX:
        nk += 1
        tk = pl.cdiv(pl.cdiv(k, nk), LANES) * LANES
    assert (nk - 1) * tk < k
    per_chunk = n // tn
    common_specs = [
        pl.BlockSpec((None, tk, tn), lambda i, j, kk: (l, kk, j)),
        pl.BlockSpec((tm, tn), lambda i, j, kk: (i, j)),
        pl.BlockSpec((None, None, 1, tn), lambda i, j, kk: (l, g_base + (i * tm) // g_rows, 0, chunk * per_chunk + j)),
    ]
    if nk == 1:
        kern = functools.partial(_mm_resid_kernel, widths=widths, coef=coef)
        a_specs = [pl.BlockSpec((tm, width), lambda i, j, kk: (i, 0)) for width in widths]
        scratch = []
    else:
        kern = functools.partial(_mm_resid_split_kernel, coef=coef, nk=nk, tk=tk, tm=tm, depth_k=k)
        a_specs = [pl.BlockSpec(memory_space=pl.ANY)]
        scratch = [pltpu.VMEM((tm, tn), F32), pltpu.VMEM((tm, tk), BF16), pltpu.VMEM((tm, k - tk), BF16),
                   pltpu.SemaphoreType.DMA((nk,))]
    return pl.pallas_call(
        kern,
        grid=(m // tm, n // tn, nk),
        in_specs=a_specs + common_specs,
        out_specs=pl.BlockSpec((tm, tn), lambda i, j, kk: (i, j)),
        out_shape=jax.ShapeDtypeStruct((m, n), F32),
        scratch_shapes=scratch,
        compiler_params=_params(("arbitrary", "arbitrary", "arbitrary")),
        name=name,
    )(*a_parts, w, x, mod4)


def _post_kernel(*refs, dims, rope, emit_state, n_alias):
    n_dqk, n_dv, n_gq, n_gk, q_lora, kv_lora = dims
    p_ref, pk_ref, gqn_ref, gkn_ref, mqn_ref, mkvn_ref = refs[:6]
    pos = 6
    if rope:
        c128, lo128, hi128, c64, lo64, hi64 = (r[...] for r in refs[pos:pos + 6])
        pos += 6
    pos += n_alias
    qkv_ref, cq_ref, ckv_ref, kpe_ref = refs[pos:pos + 4]
    pos += 4
    if emit_state:
        dk_f_ref, dv_f_ref, gk_f_ref, gv_f_ref, ckv_f_ref = refs[pos:pos + 5]

    col = 0
    for _ in range(n_dqk // LANES):
        x = p_ref[:, col:col + LANES]
        if rope:
            x = _rope(x, c64, lo64, hi64, DIFF_QK // 4)
        qkv_ref[:, col:col + LANES] = x.astype(BF16)
        col += LANES
    if emit_state:
        dk_f_ref[...] = p_ref[:, n_dqk // 2:n_dqk]
    dv = p_ref[:, col:col + n_dv]
    qkv_ref[:, col:col + n_dv] = dv.astype(BF16)
    if emit_state:
        dv_f_ref[...] = dv
    col += n_dv
    for _ in range(n_gq // LANES):
        x = _rms(p_ref[:, col:col + LANES], gqn_ref[...])
        if rope:
            x = _rope(x, c128, lo128, hi128, HEAD_DIM // 4)
        qkv_ref[:, col:col + LANES] = x.astype(BF16)
        col += LANES
    for h in range(n_gk // LANES):
        x = _rms(p_ref[:, col:col + LANES], gkn_ref[...])
        if emit_state:
            gk_f_ref[:, h * LANES:(h + 1) * LANES] = x
        if rope:
            x = _rope(x, c128, lo128, hi128, HEAD_DIM // 4)
        qkv_ref[:, col:col + LANES] = x.astype(BF16)
        col += LANES
    gv = p_ref[:, col:col + n_gk]
    qkv_ref[:, col:col + n_gk] = gv.astype(BF16)
    if emit_state:
        gv_f_ref[...] = gv
    col += n_gk
    cq_ref[...] = _rms(p_ref[:, col:col + q_lora], mqn_ref[...]).astype(BF16)
    col += q_lora
    ckv = _rms(p_ref[:, col:col + kv_lora], mkvn_ref[...])
    ckv_ref[...] = ckv.astype(BF16)
    if emit_state:
        ckv_f_ref[...] = ckv
    kpe = pk_ref[...]
    if rope:
        kpe = _rope(kpe, c64, lo64, hi64, MLA_ROPE // 4)
    kpe_ref[...] = kpe.astype(BF16)


def _post(p, pk, l, depth, dims, gqn, gkn, mqn, mkvn, tables, t_len, emit_state, state_in):
    m, n = p.shape
    n_dqk, n_dv, n_gq, n_gk, q_lora, kv_lora = dims
    n_qkv = n_dqk + n_dv + n_gq + 2 * n_gk
    rope = tables is not None
    tr = _row_tile(t_len, 256)
    nt = t_len // tr
    row_spec = lambda width: pl.BlockSpec((tr, width), lambda i: (i, 0))
    gain_spec = lambda width: pl.BlockSpec((None, 1, width), lambda i: (l, 0, 0))
    in_specs = [row_spec(n), row_spec(LANES), gain_spec(HEAD_DIM), gain_spec(HEAD_DIM), gain_spec(q_lora),
                gain_spec(kv_lora)]
    args = [p, pk, gqn, gkn, mqn, mkvn]
    if rope:
        in_specs += [pl.BlockSpec((tr, LANES), lambda i: (i % nt, 0))] * 6
        args += list(tables)
    out_widths = [(n_qkv, BF16), (q_lora, BF16), (kv_lora, BF16), (LANES, BF16)]
    out_specs = [row_spec(width) for width, _ in out_widths]
    out_shape = [jax.ShapeDtypeStruct((m, width), dt) for width, dt in out_widths]
    aliases = {}
    n_alias = 0
    if emit_state:
        state_widths = [n_dqk // 2, n_dv, n_gk, n_gk, kv_lora]
        if state_in is not None:
            n_alias = len(state_in)
            aliases = {len(args) + s: len(out_shape) + s for s in range(n_alias)}
            in_specs += [pl.BlockSpec(memory_space=pl.ANY)] * n_alias
            args += list(state_in)
        out_specs += [pl.BlockSpec((None, None, tr, width), lambda i: (i // nt, l, i % nt, 0)) for width in state_widths]
        out_shape += [jax.ShapeDtypeStruct((m // t_len, depth, t_len, width), F32) for width in state_widths]
    kern = functools.partial(_post_kernel, dims=dims, rope=rope, emit_state=emit_state, n_alias=n_alias)
    return pl.pallas_call(
        kern,
        grid=(m // tr,),
        in_specs=in_specs,
        out_specs=out_specs,
        out_shape=out_shape,
        input_output_aliases=aliases,
        compiler_params=_params(("arbitrary",)),
        name="mixer_post",
    )(*args)


def _attend(q_parts, segments, c):
    s = [_sum([_dot_nt(q, k) for q, k in zip(q_parts, ks)]) for ks, _ in segments]
    m = functools.reduce(jnp.maximum, [jnp.max(x, axis=-1, keepdims=True) for x in s])
    e = [jnp.exp2((x - m) * c) for x in s]
    den = _sum([jnp.sum(x, axis=-1, keepdims=True) for x in e])
    o = _sum([jnp.dot(x.astype(BF16), v, preferred_element_type=F32) for x, (_, v) in zip(e, segments)])
    return o / den


def _diff_attn_kernel(*refs, heads, has_cache, lam_init):
    q_ref, k_ref, v_ref = refs[:3]
    pos = 3
    if has_cache:
        kc_ref, vc_ref = refs[pos:pos + 2]
        pos += 2
    lq1, lk1, lq2, lk2, sub_ref, o_ref = refs[pos:pos + 6]
    lam = (jnp.exp(jnp.sum(lq1[...] * lk1[...], axis=-1, keepdims=True))
           - jnp.exp(jnp.sum(lq2[...] * lk2[...], axis=-1, keepdims=True)) + lam_init)
    c = DIFF_QK ** -0.5 * LOG2_E
    lane = lax.broadcasted_iota(jnp.int32, (q_ref.shape[0], LANES), 1)
    for h in range(heads):
        cols = slice(h * LANES, (h + 1) * LANES)
        q = q_ref[:, cols]
        zero = jnp.zeros_like(q)
        segments = [([k_ref[:, cols]], v_ref[:, cols])]
        if has_cache:
            segments.insert(0, ([kc_ref[:, cols].astype(BF16)], vc_ref[:, cols].astype(BF16)))
        o0 = _attend([jnp.where(lane < DIFF_QK, q, zero)], segments, c)
        o1 = _attend([jnp.where(lane >= DIFF_QK, q, zero)], segments, c)
        o = o0 - lam * o1
        o_ref[:, cols] = (_rms(o, sub_ref[...]) * (1.0 - lam_init)).astype(o_ref.dtype)


def _gqa_attn_kernel(*refs, kv_heads, group, has_cache):
    q_ref, k_ref, v_ref = refs[:3]
    pos = 3
    if has_cache:
        kc_ref, vc_ref = refs[pos:pos + 2]
        pos += 2
    o_ref = refs[pos]
    c = HEAD_DIM ** -0.5 * LOG2_E
    for kvh in range(kv_heads):
        cols = slice(kvh * LANES, (kvh + 1) * LANES)
        segments = [([k_ref[:, cols]], v_ref[:, cols])]
        if has_cache:
            segments.insert(0, ([kc_ref[:, cols].astype(BF16)], vc_ref[:, cols].astype(BF16)))
        for g in range(group):
            qcols = slice((kvh * group + g) * LANES, (kvh * group + g + 1) * LANES)
            o_ref[:, qcols] = _attend([q_ref[:, qcols]], segments, c).astype(o_ref.dtype)


def _mla_attn_kernel(*refs, heads, has_cache, rope):
    mq_ref, kv_ref, kp_ref = refs[:3]
    pos = 3
    if has_cache:
        kvc_ref, kpc_ref = refs[pos:pos + 2]
        pos += 2
    if rope:
        c64, lo64, hi64 = (r[...] for r in refs[pos:pos + 3])
        pos += 3
    o_ref = refs[pos]
    c = (MLA_NOPE + MLA_ROPE) ** -0.5 * LOG2_E
    kp = kp_ref[...]
    if has_cache:
        kpc = kpc_ref[...].astype(BF16)
    for h in range(heads):
        cols = slice(h * LANES, (h + 1) * LANES)
        kcols = slice(2 * h * LANES, (2 * h + 1) * LANES)
        vcols = slice((2 * h + 1) * LANES, (2 * h + 2) * LANES)
        qn = mq_ref[:, cols].astype(BF16)
        qp = mq_ref[:, (heads + h) * LANES:(heads + h + 1) * LANES]
        if rope:
            qp = _rope(qp, c64, lo64, hi64, MLA_ROPE // 4)
        q = jnp.concatenate([qn, qp.astype(BF16)], axis=-1)
        segments = [([jnp.concatenate([kv_ref[:, kcols], kp], axis=-1)], kv_ref[:, vcols])]
        if has_cache:
            segments.insert(0, ([jnp.concatenate([kvc_ref[:, kcols], kpc], axis=-1)], kvc_ref[:, vcols]))
        o_ref[:, cols] = _attend([q], segments, c).astype(o_ref.dtype)


def _q_tile(t):
    return 256 if t % 256 == 0 else t


def _attn_call(kern, name, batch, t, out_width, q_specs, kv_specs, cache_specs, extra_specs, args):
    tq = _q_tile(t)
    nq = t // tq
    in_specs = ([pl.BlockSpec((tq, w), lambda b, i, c=c: (b * nq + i, c)) for w, c in q_specs]
                + [pl.BlockSpec((t, w), lambda b, i, c=c: (b, c)) for w, c in kv_specs]
                + cache_specs + extra_specs)
    return pl.pallas_call(
        kern,
        grid=(batch, nq),
        in_specs=in_specs,
        out_specs=pl.BlockSpec((tq, out_width), lambda b, i: (b * nq + i, 0)),
        out_shape=jax.ShapeDtypeStruct((batch * t, out_width), BF16),
        compiler_params=_params(("arbitrary", "arbitrary")),
        name=name,
    )(*args)


def _cache_spec(l, s_cache, width, col):
    return pl.BlockSpec((None, None, s_cache, width), lambda b, i: (b, l, 0, col))


def _diff_attention(qkv, l, batch, t, heads, cache_k4, cache_v4, lparams, subln):
    has_cache = cache_k4 is not None
    w = heads * LANES
    cache_specs, args = [], [qkv, qkv, qkv]
    if has_cache:
        cache_specs = [_cache_spec(l, cache_k4.shape[2], w, 0)] * 2
        args += [cache_k4, cache_v4]
    extra = [pl.BlockSpec((None, 1, DIFF_QK), lambda b, i: (l, 0, 0))] * 4
    extra += [pl.BlockSpec((None, 1, DIFF_V), lambda b, i: (l, 0, 0))]
    args += list(lparams) + [subln]
    lam_init = 0.8 - 0.6 * math.exp(-0.3 * l)
    kern = functools.partial(_diff_attn_kernel, heads=heads, has_cache=has_cache, lam_init=lam_init)
    return _attn_call(kern, "diff_attention", batch, t, w, [(w, 0)], [(w, 1), (w, 2)], cache_specs, extra, args)


def _gqa_attention(qkv, l, batch, t, col0, q_heads, kv_heads, cache_k4, cache_v4):
    has_cache = cache_k4 is not None
    qw, kw = q_heads * HEAD_DIM, kv_heads * HEAD_DIM
    assert col0 % qw == 0 and (col0 + qw) % kw == 0
    k_blk = (col0 + qw) // kw
    cache_specs, args = [], [qkv, qkv, qkv]
    if has_cache:
        cache_specs = [_cache_spec(l, cache_k4.shape[2], kw, 0)] * 2
        args += [cache_k4, cache_v4]
    kern = functools.partial(_gqa_attn_kernel, kv_heads=kv_heads, group=q_heads // kv_heads, has_cache=has_cache)
    return _attn_call(kern, "gqa_attention", batch, t, qw, [(qw, col0 // qw)], [(kw, k_blk), (kw, k_blk + 1)],
                      cache_specs, [], args)


def _mla_attention(mq, kv, kpe, l, batch, t, heads, kv_cache, cache_kpe4, tables64):
    has_cache = kv_cache is not None
    rope = tables64 is not None
    tq = _q_tile(t)
    cache_specs, extra, args = [], [], [mq, kv, kpe]
    if has_cache:
        s_cache = cache_kpe4.shape[2]
        cache_specs = [pl.BlockSpec((s_cache, kv.shape[1]), lambda b, i: (b, 0)), _cache_spec(l, s_cache, LANES, 0)]
        args += [kv_cache, cache_kpe4]
    if rope:
        extra = [pl.BlockSpec((tq, LANES), lambda b, i: (i, 0))] * 3
        args += list(tables64)
    kern = functools.partial(_mla_attn_kernel, heads=heads, has_cache=has_cache, rope=rope)
    return _attn_call(kern, "mla_attention", batch, t, heads * MLA_V, [(mq.shape[1], 0)],
                      [(kv.shape[1], 0), (LANES, 0)], cache_specs, extra, args)


def _rope_tables(t_len, width):
    half = width // 2
    quarter = half // 2
    t = jnp.arange(t_len, dtype=jnp.int32)
    rows = (t // GRID_W).astype(F32)
    cols = (t % GRID_W).astype(F32)
    lane = jnp.arange(LANES, dtype=jnp.int32) % width
    inv = ROPE_THETA ** (-(jnp.arange(0, half, 2, dtype=F32) / half))
    inv_lane = inv[(lane % half) % quarter]
    pos = jnp.where((lane < half)[None, :], rows[:, None], cols[:, None])
    ang = pos * inv_lane[None, :]
    cos, sin = jnp.cos(ang), jnp.sin(ang)
    first = ((lane % half) < quarter)[None, :]
    zero = jnp.zeros_like(sin)
    return cos, jnp.where(first, -sin, zero), jnp.where(first, zero, sin)


def _layer(x, l, w, mod4, grp, batch, t, caches, tables, emit_state, state_in=None):
    ffn = lambda h, w_in, w_out, norm, c0, name: _matmul_resid(
        [_norm_swiglu(h, norm, l, mod4, grp, c0, c0 + 1, w_in)],
        w_out, l, h, mod4, grp, c0 + 2, 0.5, name, tn_target=512)
    x = ffn(x, w["ffn1_w_in"], w["ffn1_w_out"], w["norm_ffn1"], 0, "ffn1_out")

    u = _normmod(x, w["norm_mix"], l, mod4, grp, 3, 4)
    dims = w["dims"]
    n_dqk, n_dv, n_gq, n_gk, q_lora, kv_lora = dims
    n_main = n_dqk + n_dv + n_gq + 2 * n_gk + q_lora + kv_lora
    p = _matmul_t(u, w["w_in_t"], l, F32, "mixer_in", n_main, tn_target=512)
    pk = _matmul_t_tail(u, w["w_in_t"], l, n_main, "mixer_in_kpe")
    post = _post(p, pk, l, w["ffn1_w_in"].shape[0], dims, w["gqa_q_norm"], w["gqa_k_norm"], w["mla_q_norm"],
                 w["mla_kv_norm"], tables[0] + tables[1] if tables is not None else None, t, emit_state, state_in)
    qkv, cqn, ckvn, kpe = post[:4]

    diff_heads = n_dv // DIFF_V
    q_heads, kv_heads = n_gq // HEAD_DIM, n_gk // HEAD_DIM
    mla_heads = w["mla_w_ukv"].shape[-1] // (MLA_NOPE + MLA_V)
    has_cache = caches is not None
    o_diff = _diff_attention(qkv, l, batch, t, diff_heads,
                             caches["diff_k"] if has_cache else None, caches["diff_v"] if has_cache else None,
                             w["diff_l"], w["diff_subln"])
    o_gqa = _gqa_attention(qkv, l, batch, t, n_dqk + n_dv, q_heads, kv_heads,
                           caches["gqa_k"] if has_cache else None, caches["gqa_v"] if has_cache else None)
    mq = _matmul(cqn, w["mla_w_uq"], l, F32 if tables is not None else BF16, "mla_q_up", tn_target=1024)
    kv = _matmul(ckvn, w["mla_w_ukv"], l, BF16, "mla_kv_up", tn_target=w["mla_w_ukv"].shape[-1])
    kv_cache = _matmul_cache(caches["mla_ckv"], l, w["mla_w_ukv"], l, BF16, "mla_kv_up_cache") if has_cache else None
    o_mla = _mla_attention(mq, kv, kpe, l, batch, t, mla_heads, kv_cache,
                           caches["mla_kpe"] if has_cache else None, tables[1] if tables is not None else None)
    x = _matmul_resid([o_diff, o_gqa, o_mla], w["w_out"], l, x, mod4, grp, 5, 1.0, "mixer_out", tn_target=512)

    x = ffn(x, w["ffn2_w_in"], w["ffn2_w_out"], w["norm_ffn2"], 6, "ffn2_out")
    state = (list(post[4:]), pk[:, :MLA_ROPE]) if emit_state else None
    return x, state


def kernel(x_prompt, x_sample, cache_diff_k, cache_diff_v, cache_gqa_k, cache_gqa_v, cache_mla_ckv, cache_mla_kpe, c, c_ctx, w_mod, b_mod, norm_ffn1, norm_mix, norm_ffn2, ffn1_w_in, ffn1_w_out, ffn2_w_in, ffn2_w_out, w_in, w_out, diff_lq1, diff_lk1, diff_lq2, diff_lk2, diff_subln, gqa_q_norm, gqa_k_norm, mla_q_norm, mla_kv_norm, mla_w_uq, mla_w_ukv, final_norm):
    batch, seq, d = x_prompt.shape
    dec_batch, dec_seq, _ = x_sample.shape
    depth = w_mod.shape[0]
    past = cache_diff_k.shape[2]
    diff_heads = cache_diff_k.shape[3]
    kv_heads = cache_gqa_k.shape[3]
    q_lora, kv_lora = mla_q_norm.shape[-1], mla_kv_norm.shape[-1]
    mla_heads = mla_w_ukv.shape[-1] // (MLA_NOPE + MLA_V)
    n_dqk = 2 * diff_heads * 2 * DIFF_QK
    n_dv = diff_heads * DIFF_V
    n_gk = kv_heads * HEAD_DIM
    n_gq = w_in.shape[-1] - (n_dqk + n_dv + 2 * n_gk + q_lora + kv_lora + MLA_ROPE)
    dims = (n_dqk, n_dv, n_gq, n_gk, q_lora, kv_lora)
    assert dec_batch + 1 <= MOD_ROWS

    cond = jnp.zeros((MOD_ROWS, d), F32).at[0].set(c_ctx).at[1:1 + dec_batch].set(c)
    mod4 = _modulation(cond, w_mod, b_mod).reshape(depth, MOD_ROWS, 1, N_MOD * d)

    uq = mla_w_uq.reshape(depth, q_lora, mla_heads, MLA_NOPE + MLA_ROPE)
    uq_pe = jnp.pad(uq[..., MLA_NOPE:], ((0, 0), (0, 0), (0, 0), (0, LANES - MLA_ROPE)))
    uq = jnp.concatenate([uq[..., :MLA_NOPE].reshape(depth, q_lora, -1), uq_pe.reshape(depth, q_lora, -1)], axis=-1)
    cache_kpe = jnp.pad(cache_mla_kpe, ((0, 0), (0, 0), (0, 0), (0, LANES - MLA_ROPE)))
    w_in_t = jnp.swapaxes(w_in, 1, 2)

    r3 = lambda a: a.reshape(depth, 1, a.shape[-1])
    weights = {
        "ffn1_w_in": ffn1_w_in, "ffn1_w_out": ffn1_w_out, "ffn2_w_in": ffn2_w_in, "ffn2_w_out": ffn2_w_out,
        "w_in_t": w_in_t, "w_out": w_out, "mla_w_uq": uq, "mla_w_ukv": mla_w_ukv,
        "norm_ffn1": r3(norm_ffn1), "norm_mix": r3(norm_mix), "norm_ffn2": r3(norm_ffn2),
        "gqa_q_norm": r3(gqa_q_norm), "gqa_k_norm": r3(gqa_k_norm),
        "mla_q_norm": r3(mla_q_norm), "mla_kv_norm": r3(mla_kv_norm),
        "diff_l": (r3(diff_lq1), r3(diff_lk1), r3(diff_lq2), r3(diff_lk2)), "diff_subln": r3(diff_subln),
        "dims": dims,
    }
    caches = {
        "diff_k": cache_diff_k.reshape(dec_batch, depth, past, -1),
        "diff_v": cache_diff_v.reshape(dec_batch, depth, past, -1),
        "gqa_k": cache_gqa_k.reshape(dec_batch, depth, past, -1),
        "gqa_v": cache_gqa_v.reshape(dec_batch, depth, past, -1),
        "mla_ckv": cache_mla_ckv, "mla_kpe": cache_kpe,
    }
    tables = (_rope_tables(dec_seq, HEAD_DIM), _rope_tables(dec_seq, DIFF_QK))

    h = x_prompt.reshape(batch * seq, d)
    state, kpes = None, []
    for l in range(depth):
        h, (state, kpe_l) = _layer(h, l, weights, mod4, (0, batch * seq), batch, seq, None, None, True, state)
        kpes.append(kpe_l.reshape(batch, seq, MLA_ROPE))
    y_prompt = _final_norm(h, final_norm).reshape(batch, seq, d)

    h = x_sample.reshape(dec_batch * dec_seq, d)
    for l in range(depth):
        h, _ = _layer(h, l, weights, mod4, (1, dec_seq), dec_batch, dec_seq, caches, tables, False)
    y_sample = _final_norm(h, final_norm).reshape(dec_batch, dec_seq, d)

    dk, dv, gk, gv, ckv = state
    lead = (batch, depth, seq)
    return (y_prompt, y_sample,
            dk.reshape(lead + (diff_heads, 2, DIFF_QK)), dv.reshape(lead + (diff_heads, DIFF_V)),
            gk.reshape(lead + (kv_heads, HEAD_DIM)), gv.reshape(lead + (kv_heads, HEAD_DIM)),
            ckv, jnp.stack(kpes, axis=1))
```

```python
import functools
import math
import operator

import jax
import jax.numpy as jnp
from jax import lax
from jax.experimental import pallas as pl
from jax.experimental.pallas import tpu as pltpu

F32 = jnp.float32
BF16 = jnp.bfloat16

NORM_EPS = 1e-6
ROPE_THETA = 10000.0
GRID_W = 64
N_MOD = 9
DIFF_QK = 64
DIFF_V = 128
HEAD_DIM = 128
MLA_NOPE = 128
MLA_ROPE = 64
MLA_V = 128
LANES = 128
MOD_ROWS = 16
VMEM_LIMIT_BYTES = 56 * 1024 * 1024
W_TILE_BYTES_MAX = 6 * 1024 * 1024
LOG2_E = 1.4426950408889634


def _params(sem):
    return pltpu.CompilerParams(dimension_semantics=sem, vmem_limit_bytes=VMEM_LIMIT_BYTES)


def _silu(x):
    return x / (1.0 + jnp.exp(-x))


def _rms(x, gain):
    y = x * lax.rsqrt(jnp.mean(x * x, axis=-1, keepdims=True) + NORM_EPS)
    return y * gain


def _rope(x, cos, sin_lo, sin_hi, shift):
    return (x * cos + pltpu.roll(x, LANES - shift, 1) * sin_lo + pltpu.roll(x, shift, 1) * sin_hi)


_NT = (((1,), (1,)), ((), ()))


def _dot_nt(a, b):
    return lax.dot_general(a, b, _NT, preferred_element_type=F32)


def _sum(xs):
    return functools.reduce(operator.add, xs)


def _mod_kernel(c_ref, w_ref, b_ref, o_ref):
    a = _silu(c_ref[...]).astype(BF16)
    o_ref[...] = jnp.dot(a, w_ref[...].astype(BF16), preferred_element_type=F32) + b_ref[...]


def _modulation(cond, w_mod, b_mod):
    depth, d, n = w_mod.shape
    tn = 512 if n % 512 == 0 else n
    return pl.pallas_call(
        _mod_kernel,
        grid=(depth, n // tn),
        in_specs=[
            pl.BlockSpec((MOD_ROWS, d), lambda l, j: (0, 0)),
            pl.BlockSpec((None, d, tn), lambda l, j: (l, 0, j)),
            pl.BlockSpec((None, 1, tn), lambda l, j: (l, 0, j)),
        ],
        out_specs=pl.BlockSpec((None, MOD_ROWS, tn), lambda l, j: (l, 0, j)),
        out_shape=jax.ShapeDtypeStruct((depth, MOD_ROWS, n), F32),
        compiler_params=_params(("arbitrary", "arbitrary")),
        name="modulation",
    )(cond, w_mod, b_mod.reshape(depth, 1, n))


NORM_ROWS = 16


def _norm_rows(x_ref, o_ref, gain, shift):
    def body(r, carry):
        rows = pl.ds(pl.multiple_of(r * NORM_ROWS, NORM_ROWS), NORM_ROWS)
        x = x_ref[rows, :]
        y = x * lax.rsqrt(jnp.mean(x * x, axis=-1, keepdims=True) + NORM_EPS) * gain
        o_ref[rows, :] = (y if shift is None else y + shift).astype(o_ref.dtype)
        return carry

    lax.fori_loop(0, x_ref.shape[0] // NORM_ROWS, body, 0, unroll=8)


def _normmod_kernel(x_ref, g_ref, sh_ref, sc_ref, o_ref):
    _norm_rows(x_ref, o_ref, g_ref[...] * (1.0 + sc_ref[...]), sh_ref[...])


def _norm_kernel(x_ref, g_ref, o_ref):
    _norm_rows(x_ref, o_ref, g_ref[...], None)


def _row_tile(m, target):
    t = min(m, target)
    assert m % t == 0
    return t


def _normmod(x, gain3, l, mod4, grp, chunk_shift, chunk_scale, rows=None):
    m, d = x.shape
    m = m if rows is None else rows
    g_base, g_rows = grp
    tr = _row_tile(min(g_rows, m), 512)
    return pl.pallas_call(
        _normmod_kernel,
        grid=(m // tr,),
        in_specs=[
            pl.BlockSpec((tr, d), lambda i: (i, 0)),
            pl.BlockSpec((None, 1, d), lambda i: (l, 0, 0)),
            pl.BlockSpec((None, None, 1, d), lambda i: (l, g_base + (i * tr) // g_rows, 0, chunk_shift)),
            pl.BlockSpec((None, None, 1, d), lambda i: (l, g_base + (i * tr) // g_rows, 0, chunk_scale)),
        ],
        out_specs=pl.BlockSpec((tr, d), lambda i: (i, 0)),
        out_shape=jax.ShapeDtypeStruct((m, d), BF16),
        compiler_params=_params(("arbitrary",)),
        name="normmod",
    )(x, gain3, mod4, mod4)


def _final_norm(x, gain):
    m, d = x.shape
    tr = _row_tile(m, 512)
    return pl.pallas_call(
        _norm_kernel,
        grid=(m // tr,),
        in_specs=[pl.BlockSpec((tr, d), lambda i: (i, 0)), pl.BlockSpec((1, d), lambda i: (0, 0))],
        out_specs=pl.BlockSpec((tr, d), lambda i: (i, 0)),
        out_shape=jax.ShapeDtypeStruct((m, d), F32),
        compiler_params=_params(("arbitrary",)),
        name="final_norm",
    )(x, gain.reshape(1, d))


def _mm_kernel(a_ref, w_ref, o_ref):
    a = a_ref[...].astype(BF16)
    o_ref[...] = jnp.dot(a, w_ref[...].astype(BF16), preferred_element_type=F32).astype(o_ref.dtype)


def _mm_t_kernel(a_ref, wt_ref, o_ref):
    o_ref[...] = _dot_nt(a_ref[...], wt_ref[...].astype(BF16)).astype(o_ref.dtype)


def _mm_t_tail_kernel(a_ref, wt_ref, o_ref, *, valid):
    wt = wt_ref[...]
    row = lax.broadcasted_iota(jnp.int32, wt.shape, 0)
    wt = jnp.where(row < valid, wt, 0.0).astype(BF16)
    o_ref[...] = _dot_nt(a_ref[...], wt)


def _swiglu_and_norm_next(a_ref, nxt_ref, rows, xc_ref, gain, shift, wg_ref, wu_ref, o_ref):
    x = xc_ref[...]
    y = x * lax.rsqrt(jnp.mean(x * x, axis=-1, keepdims=True) + NORM_EPS) * gain + shift
    nxt_ref[rows, :] = y.astype(BF16)
    a = a_ref[...]
    g = jnp.dot(a, wg_ref[...].astype(BF16), preferred_element_type=F32)
    u = jnp.dot(a, wu_ref[...].astype(BF16), preferred_element_type=F32)
    o_ref[...] = (_silu(g) * u).astype(o_ref.dtype)


def _mm_swiglu_fused_kernel(a0_ref, xc_ref, g_ref, sh_ref, sc_ref, wg_ref, wu_ref, o_ref, buf0, buf1, *,
                            n_chunks, chunk):
    i, j = pl.program_id(0), pl.program_id(1)

    @pl.when((i == 0) & (j == 0))
    def _():
        buf0[...] = a0_ref[...]

    gain = g_ref[...] * (1.0 + sc_ref[...])
    rows = pl.ds(pl.multiple_of(jnp.minimum(j, n_chunks - 1) * chunk, chunk), chunk)
    args = (rows, xc_ref, gain, sh_ref[...], wg_ref, wu_ref, o_ref)

    @pl.when(i % 2 == 0)
    def _():
        _swiglu_and_norm_next(buf0, buf1, *args)

    @pl.when(i % 2 == 1)
    def _():
        _swiglu_and_norm_next(buf1, buf0, *args)


def _mm_resid_kernel(*refs, widths, coef):
    n_a = len(widths)
    a_refs = refs[:n_a]
    w_ref, x_ref, gate_ref, o_ref = refs[n_a:]
    row = 0
    parts = []
    for a_ref, width in zip(a_refs, widths):
        parts.append(jnp.dot(a_ref[...], w_ref[row:row + width, :].astype(BF16), preferred_element_type=F32))
        row += width
    o_ref[...] = x_ref[...] + (coef * gate_ref[...]) * _sum(parts)


def _mm_resid_split_kernel(a_hbm, w_ref, x_ref, gate_ref, o_ref, acc_ref, head_buf, rest_buf, sems, *,
                           coef, nk, tk, tm, depth_k):
    i, j, k = pl.program_id(0), pl.program_id(1), pl.program_id(2)
    n_i, n_j = pl.num_programs(0), pl.num_programs(1)
    width = lambda kk: min(tk, depth_k - kk * tk)

    def chunk_copy(tile, kk):
        src = a_hbm.at[pl.ds(tile * tm, tm), pl.ds(kk * tk, width(kk))]
        dst = head_buf if kk == 0 else rest_buf.at[:, pl.ds((kk - 1) * tk, width(kk))]
        return pltpu.make_async_copy(src, dst, sems.at[kk])

    @pl.when((j == 0) & (k == 0))
    def _():
        @pl.when(i == 0)
        def _():
            chunk_copy(0, 0).start()

        for kk in range(1, nk):
            chunk_copy(i, kk).start()
        chunk_copy(i, 0).wait()

    for kk in range(1, nk):
        @pl.when((j == 0) & (k == kk))
        def _(kk=kk):
            chunk_copy(i, kk).wait()

    @pl.when((j == n_j - 1) & (k == 1) & (i + 1 < n_i))
    def _():
        chunk_copy(i + 1, 0).start()

    for kk in range(nk):
        @pl.when(k == kk)
        def _(kk=kk):
            rows = width(kk)
            a = head_buf[...] if kk == 0 else rest_buf[:, (kk - 1) * tk:(kk - 1) * tk + rows]
            part = jnp.dot(a, w_ref[0:rows, :].astype(BF16), preferred_element_type=F32)
            if kk > 0:
                part = acc_ref[...] + part
            if kk == nk - 1:
                o_ref[...] = x_ref[...] + (coef * gate_ref[...]) * part
            else:
                acc_ref[...] = part


def _col_tile(n, target=256):
    return target if n % target == 0 else n


def _matmul_t_tail(a, wt, l, row0, name):
    m, k = a.shape
    valid = wt.shape[1] - row0
    assert row0 % LANES == 0 and 0 < valid <= LANES
    tm = _row_tile(m, 1024)
    return pl.pallas_call(
        functools.partial(_mm_t_tail_kernel, valid=valid),
        grid=(m // tm,),
        in_specs=[
            pl.BlockSpec((tm, k), lambda i: (i, 0)),
            pl.BlockSpec((None, LANES, k), lambda i: (l, row0 // LANES, 0)),
        ],
        out_specs=pl.BlockSpec((tm, LANES), lambda i: (i, 0)),
        out_shape=jax.ShapeDtypeStruct((m, LANES), F32),
        compiler_params=_params(("arbitrary",)),
        name=name,
    )(a, wt)


def _matmul_t(a, wt, l, out_dtype, name, n, tn_target=256):
    m, k = a.shape
    tm = _row_tile(m, 1024)
    tn = _col_tile(n, tn_target)
    return pl.pallas_call(
        _mm_t_kernel,
        grid=(m // tm, n // tn),
        in_specs=[
            pl.BlockSpec((tm, k), lambda i, j: (i, 0)),
            pl.BlockSpec((None, tn, k), lambda i, j: (l, j, 0)),
        ],
        out_specs=pl.BlockSpec((tm, tn), lambda i, j: (i, j)),
        out_shape=jax.ShapeDtypeStruct((m, n), out_dtype),
        compiler_params=_params(("arbitrary", "arbitrary")),
        name=name,
    )(a, wt)


def _matmul(a, w, l, out_dtype, name, n=None, tn_target=256):
    m, k = a.shape
    n = w.shape[-1] if n is None else n
    tm = _row_tile(m, 1024)
    tn = _col_tile(n, tn_target)
    return pl.pallas_call(
        _mm_kernel,
        grid=(m // tm, n // tn),
        in_specs=[
            pl.BlockSpec((tm, k), lambda i, j: (i, 0)),
            pl.BlockSpec((None, k, tn), lambda i, j: (l, 0, j)),
        ],
        out_specs=pl.BlockSpec((tm, tn), lambda i, j: (i, j)),
        out_shape=jax.ShapeDtypeStruct((m, n), out_dtype),
        compiler_params=_params(("arbitrary", "arbitrary")),
        name=name,
    )(a, w)


def _matmul_cache(a4, l, w, lw, out_dtype, name):
    b, _, s, k = a4.shape
    n = w.shape[-1]
    tn = n
    return pl.pallas_call(
        _mm_kernel,
        grid=(b, n // tn),
        in_specs=[
            pl.BlockSpec((None, None, s, k), lambda i, j: (i, l, 0, 0)),
            pl.BlockSpec((None, k, tn), lambda i, j: (lw, 0, j)),
        ],
        out_specs=pl.BlockSpec((s, tn), lambda i, j: (i, j)),
        out_shape=jax.ShapeDtypeStruct((b * s, n), out_dtype),
        compiler_params=_params(("arbitrary", "arbitrary")),
        name=name,
    )(a4, w)


def _norm_swiglu(x, gain3, l, mod4, grp, chunk_shift, chunk_scale, w):
    m, k = x.shape
    f = w.shape[-1] // 2
    g_base, g_rows = grp
    tm = _row_tile(g_rows, 1024)
    tn = _col_tile(f)
    n_i, nj = m // tm, f // tn
    chunk = 16
    while chunk * nj < tm or tm % chunk:
        chunk += 16
    n_chunks = tm // chunk
    a0 = _normmod(x, gain3, l, mod4, grp, chunk_shift, chunk_scale, rows=tm)
    nxt = lambda i: jnp.minimum(i + 1, n_i - 1)
    mod_spec = lambda c: pl.BlockSpec((None, None, 1, k), lambda i, j: (l, g_base + (nxt(i) * tm) // g_rows, 0, c))
    kern = functools.partial(_mm_swiglu_fused_kernel, n_chunks=n_chunks, chunk=chunk)
    return pl.pallas_call(
        kern,
        grid=(n_i, nj),
        in_specs=[
            pl.BlockSpec((tm, k), lambda i, j: (0, 0), pipeline_mode=pl.Buffered(1)),
            pl.BlockSpec((chunk, k), lambda i, j: (nxt(i) * n_chunks + jnp.minimum(j, n_chunks - 1), 0)),
            pl.BlockSpec((None, 1, k), lambda i, j: (l, 0, 0)),
            mod_spec(chunk_shift),
            mod_spec(chunk_scale),
            pl.BlockSpec((None, k, tn), lambda i, j: (l, 0, j)),
            pl.BlockSpec((None, k, tn), lambda i, j: (l, 0, j + nj)),
        ],
        out_specs=pl.BlockSpec((tm, tn), lambda i, j: (i, j)),
        out_shape=jax.ShapeDtypeStruct((m, f), BF16),
        scratch_shapes=[pltpu.VMEM((tm, k), BF16)] * 2,
        compiler_params=_params(("arbitrary", "arbitrary")),
        name="ffn_in",
    )(a0, x, gain3, mod4, mod4, w, w)


def _matmul_resid(a_parts, w, l, x, mod4, grp, chunk, coef, name, tn_target=256):
    m = a_parts[0].shape[0]
    widths = tuple(a.shape[1] for a in a_parts)
    k = sum(widths)
    n = w.shape[-1]
    g_base, g_rows = grp
    tm = _row_tile(g_rows, 1024)
    tn = _col_tile(n, tn_target)
    nk, tk = 1, k
    while len(a_parts) == 1 and tk * tn * 4 > W_TILE_BYTES_MAX:
        nk += 1
        tk = pl.cdiv(pl.cdiv(k, nk), LANES) * LANES
    assert (nk - 1) * tk < k
    per_chunk = n // tn
    common_specs = [
        pl.BlockSpec((None, tk, tn), lambda i, j, kk: (l, kk, j)),
        pl.BlockSpec((tm, tn), lambda i, j, kk: (i, j)),
        pl.BlockSpec((None, None, 1, tn), lambda i, j, kk: (l, g_base + (i * tm) // g_rows, 0, chunk * per_chunk + j)),
    ]
    if nk == 1:
        kern = functools.partial(_mm_resid_kernel, widths=widths, coef=coef)
        a_specs = [pl.BlockSpec((tm, width), lambda i, j, kk: (i, 0)) for width in widths]
        scratch = []
    else:
        kern = functools.partial(_mm_resid_split_kernel, coef=coef, nk=nk, tk=tk, tm=tm, depth_k=k)
        a_specs = [pl.BlockSpec(memory_space=pl.ANY)]
        scratch = [pltpu.VMEM((tm, tn), F32), pltpu.VMEM((tm, tk), BF16), pltpu.VMEM((tm, k - tk), BF16),
                   pltpu.SemaphoreType.DMA((nk,))]
    return pl.pallas_call(
        kern,
        grid=(m // tm, n // tn, nk),
        in_specs=a_specs + common_specs,
        out_specs=pl.BlockSpec((tm, tn), lambda i, j, kk: (i, j)),
        out_shape=jax.ShapeDtypeStruct((m, n), F32),
        scratch_shapes=scratch,
        compiler_params=_params(("arbitrary", "arbitrary", "arbitrary")),
        name=name,
    )(*a_parts, w, x, mod4)


def _post_kernel(*refs, dims, rope, emit_state, n_alias):
    n_dqk, n_dv, n_gq, n_gk, q_lora, kv_lora = dims
    p_ref, pk_ref, gqn_ref, gkn_ref, mqn_ref, mkvn_ref = refs[:6]
    pos = 6
    if rope:
        c128, lo128, hi128, c64, lo64, hi64 = (r[...] for r in refs[pos:pos + 6])
        pos += 6
    pos += n_alias
    qkv_ref, cq_ref, ckv_ref, kpe_ref = refs[pos:pos + 4]
    pos += 4
    if emit_state:
        dk_f_ref, dv_f_ref, gk_f_ref, gv_f_ref, ckv_f_ref = refs[pos:pos + 5]

    col = 0
    for _ in range(n_dqk // LANES):
        x = p_ref[:, col:col + LANES]
        if rope:
            x = _rope(x, c64, lo64, hi64, DIFF_QK // 4)
        qkv_ref[:, col:col + LANES] = x.astype(BF16)
        col += LANES
    if emit_state:
        dk_f_ref[...] = p_ref[:, n_dqk // 2:n_dqk]
    dv = p_ref[:, col:col + n_dv]
    qkv_ref[:, col:col + n_dv] = dv.astype(BF16)
    if emit_state:
        dv_f_ref[...] = dv
    col += n_dv
    for _ in range(n_gq // LANES):
        x = _rms(p_ref[:, col:col + LANES], gqn_ref[...])
        if rope:
            x = _rope(x, c128, lo128, hi128, HEAD_DIM // 4)
        qkv_ref[:, col:col + LANES] = x.astype(BF16)
        col += LANES
    for h in range(n_gk // LANES):
        x = _rms(p_ref[:, col:col + LANES], gkn_ref[...])
        if emit_state:
            gk_f_ref[:, h * LANES:(h + 1) * LANES] = x
        if rope:
            x = _rope(x, c128, lo128, hi128, HEAD_DIM // 4)
        qkv_ref[:, col:col + LANES] = x.astype(BF16)
        col += LANES
    gv = p_ref[:, col:col + n_gk]
    qkv_ref[:, col:col + n_gk] = gv.astype(BF16)
    if emit_state:
        gv_f_ref[...] = gv
    col += n_gk
    cq_ref[...] = _rms(p_ref[:, col:col + q_lora], mqn_ref[...]).astype(BF16)
    col += q_lora
    ckv = _rms(p_ref[:, col:col + kv_lora], mkvn_ref[...])
    ckv_ref[...] = ckv.astype(BF16)
    if emit_state:
        ckv_f_ref[...] = ckv
    kpe = pk_ref[...]
    if rope:
        kpe = _rope(kpe, c64, lo64, hi64, MLA_ROPE // 4)
    kpe_ref[...] = kpe.astype(BF16)


def _post(p, pk, l, depth, dims, gqn, gkn, mqn, mkvn, tables, t_len, emit_state, state_in):
    m, n = p.shape
    n_dqk, n_dv, n_gq, n_gk, q_lora, kv_lora = dims
    n_qkv = n_dqk + n_dv + n_gq + 2 * n_gk
    rope = tables is not None
    tr = _row_tile(t_len, 256)
    nt = t_len // tr
    row_spec = lambda width: pl.BlockSpec((tr, width), lambda i: (i, 0))
    gain_spec = lambda width: pl.BlockSpec((None, 1, width), lambda i: (l, 0, 0))
    in_specs = [row_spec(n), row_spec(LANES), gain_spec(HEAD_DIM), gain_spec(HEAD_DIM), gain_spec(q_lora),
                gain_spec(kv_lora)]
    args = [p, pk, gqn, gkn, mqn, mkvn]
    if rope:
        in_specs += [pl.BlockSpec((tr, LANES), lambda i: (i % nt, 0))] * 6
        args += list(tables)
    out_widths = [(n_qkv, BF16), (q_lora, BF16), (kv_lora, BF16), (LANES, BF16)]
    out_specs = [row_spec(width) for width, _ in out_widths]
    out_shape = [jax.ShapeDtypeStruct((m, width), dt) for width, dt in out_widths]
    aliases = {}
    n_alias = 0
    if emit_state:
        state_widths = [n_dqk // 2, n_dv, n_gk, n_gk, kv_lora]
        if state_in is not None:
            n_alias = len(state_in)
            aliases = {len(args) + s: len(out_shape) + s for s in range(n_alias)}
            in_specs += [pl.BlockSpec(memory_space=pl.ANY)] * n_alias
            args += list(state_in)
        out_specs += [pl.BlockSpec((None, None, tr, width), lambda i: (i // nt, l, i % nt, 0)) for width in state_widths]
        out_shape += [jax.ShapeDtypeStruct((m // t_len, depth, t_len, width), F32) for width in state_widths]
    kern = functools.partial(_post_kernel, dims=dims, rope=rope, emit_state=emit_state, n_alias=n_alias)
    return pl.pallas_call(
        kern,
        grid=(m // tr,),
        in_specs=in_specs,
        out_specs=out_specs,
        out_shape=out_shape,
        input_output_aliases=aliases,
        compiler_params=_params(("arbitrary",)),
        name="mixer_post",
    )(*args)


def _attend(q_parts, segments, c):
    s = [_sum([_dot_nt(q, k) for q, k in zip(q_parts, ks)]) for ks, _ in segments]
    m = functools.reduce(jnp.maximum, [jnp.max(x, axis=-1, keepdims=True) for x in s])
    e = [jnp.exp2((x - m) * c) for x in s]
    den = _sum([jnp.sum(x, axis=-1, keepdims=True) for x in e])
    o = _sum([jnp.dot(x.astype(BF16), v, preferred_element_type=F32) for x, (_, v) in zip(e, segments)])
    return o / den


def _diff_attn_kernel(*refs, heads, has_cache, lam_init):
    q_ref, k_ref, v_ref = refs[:3]
    pos = 3
    if has_cache:
        kc_ref, vc_ref = refs[pos:pos + 2]
        pos += 2
    lq1, lk1, lq2, lk2, sub_ref, o_ref = refs[pos:pos + 6]
    lam = (jnp.exp(jnp.sum(lq1[...] * lk1[...], axis=-1, keepdims=True))
           - jnp.exp(jnp.sum(lq2[...] * lk2[...], axis=-1, keepdims=True)) + lam_init)
    c = DIFF_QK ** -0.5 * LOG2_E
    lane = lax.broadcasted_iota(jnp.int32, (q_ref.shape[0], LANES), 1)
    for h in range(heads):
        cols = slice(h * LANES, (h + 1) * LANES)
        q = q_ref[:, cols]
        zero = jnp.zeros_like(q)
        segments = [([k_ref[:, cols]], v_ref[:, cols])]
        if has_cache:
            segments.insert(0, ([kc_ref[:, cols].astype(BF16)], vc_ref[:, cols].astype(BF16)))
        o0 = _attend([jnp.where(lane < DIFF_QK, q, zero)], segments, c)
        o1 = _attend([jnp.where(lane >= DIFF_QK, q, zero)], segments, c)
        o = o0 - lam * o1
        o_ref[:, cols] = (_rms(o, sub_ref[...]) * (1.0 - lam_init)).astype(o_ref.dtype)


def _gqa_attn_kernel(*refs, kv_heads, group, has_cache):
    q_ref, k_ref, v_ref = refs[:3]
    pos = 3
    if has_cache:
        kc_ref, vc_ref = refs[pos:pos + 2]
        pos += 2
    o_ref = refs[pos]
    c = HEAD_DIM ** -0.5 * LOG2_E
    for kvh in range(kv_heads):
        cols = slice(kvh * LANES, (kvh + 1) * LANES)
        segments = [([k_ref[:, cols]], v_ref[:, cols])]
        if has_cache:
            segments.insert(0, ([kc_ref[:, cols].astype(BF16)], vc_ref[:, cols].astype(BF16)))
        for g in range(group):
            qcols = slice((kvh * group + g) * LANES, (kvh * group + g + 1) * LANES)
            o_ref[:, qcols] = _attend([q_ref[:, qcols]], segments, c).astype(o_ref.dtype)


def _mla_attn_kernel(*refs, heads, has_cache, rope):
    mq_ref, kv_ref, kp_ref = refs[:3]
    pos = 3
    if has_cache:
        kvc_ref, kpc_ref = refs[pos:pos + 2]
        pos += 2
    if rope:
        c64, lo64, hi64 = (r[...] for r in refs[pos:pos + 3])
        pos += 3
    o_ref = refs[pos]
    c = (MLA_NOPE + MLA_ROPE) ** -0.5 * LOG2_E
    kp = kp_ref[...]
    if has_cache:
        kpc = kpc_ref[...].astype(BF16)
    for h in range(heads):
        cols = slice(h * LANES, (h + 1) * LANES)
        kcols = slice(2 * h * LANES, (2 * h + 1) * LANES)
        vcols = slice((2 * h + 1) * LANES, (2 * h + 2) * LANES)
        qn = mq_ref[:, cols].astype(BF16)
        qp = mq_ref[:, (heads + h) * LANES:(heads + h + 1) * LANES]
        if rope:
            qp = _rope(qp, c64, lo64, hi64, MLA_ROPE // 4)
        q = jnp.concatenate([qn, qp.astype(BF16)], axis=-1)
        segments = [([jnp.concatenate([kv_ref[:, kcols], kp], axis=-1)], kv_ref[:, vcols])]
        if has_cache:
            segments.insert(0, ([jnp.concatenate([kvc_ref[:, kcols], kpc], axis=-1)], kvc_ref[:, vcols]))
        o_ref[:, cols] = _attend([q], segments, c).astype(o_ref.dtype)


def _q_tile(t, target=256):
    return target if t % target == 0 else t


def _attn_call(kern, name, batch, t, out_width, q_specs, kv_specs, cache_specs, extra_specs, args, tq_target=256):
    tq = _q_tile(t, tq_target)
    nq = t // tq
    in_specs = ([pl.BlockSpec((tq, w), lambda b, i, c=c: (b * nq + i, c)) for w, c in q_specs]
                + [pl.BlockSpec((t, w), lambda b, i, c=c: (b, c)) for w, c in kv_specs]
                + cache_specs + extra_specs)
    return pl.pallas_call(
        kern,
        grid=(batch, nq),
        in_specs=in_specs,
        out_specs=pl.BlockSpec((tq, out_width), lambda b, i: (b * nq + i, 0)),
        out_shape=jax.ShapeDtypeStruct((batch * t, out_width), BF16),
        compiler_params=_params(("arbitrary", "arbitrary")),
        name=name,
    )(*args)


def _cache_spec(l, s_cache, width, col):
    return pl.BlockSpec((None, None, s_cache, width), lambda b, i: (b, l, 0, col))


def _diff_attention(qkv, l, batch, t, heads, cache_k4, cache_v4, lparams, subln):
    has_cache = cache_k4 is not None
    w = heads * LANES
    cache_specs, args = [], [qkv, qkv, qkv]
    if has_cache:
        cache_specs = [_cache_spec(l, cache_k4.shape[2], w, 0)] * 2
        args += [cache_k4, cache_v4]
    extra = [pl.BlockSpec((None, 1, DIFF_QK), lambda b, i: (l, 0, 0))] * 4
    extra += [pl.BlockSpec((None, 1, DIFF_V), lambda b, i: (l, 0, 0))]
    args += list(lparams) + [subln]
    lam_init = 0.8 - 0.6 * math.exp(-0.3 * l)
    kern = functools.partial(_diff_attn_kernel, heads=heads, has_cache=has_cache, lam_init=lam_init)
    return _attn_call(kern, "diff_attention", batch, t, w, [(w, 0)], [(w, 1), (w, 2)], cache_specs, extra, args,
                      tq_target=512)


def _gqa_attention(qkv, l, batch, t, col0, q_heads, kv_heads, cache_k4, cache_v4):
    has_cache = cache_k4 is not None
    qw, kw = q_heads * HEAD_DIM, kv_heads * HEAD_DIM
    assert col0 % qw == 0 and (col0 + qw) % kw == 0
    k_blk = (col0 + qw) // kw
    cache_specs, args = [], [qkv, qkv, qkv]
    if has_cache:
        cache_specs = [_cache_spec(l, cache_k4.shape[2], kw, 0)] * 2
        args += [cache_k4, cache_v4]
    kern = functools.partial(_gqa_attn_kernel, kv_heads=kv_heads, group=q_heads // kv_heads, has_cache=has_cache)
    return _attn_call(kern, "gqa_attention", batch, t, qw, [(qw, col0 // qw)], [(kw, k_blk), (kw, k_blk + 1)],
                      cache_specs, [], args, tq_target=512)


def _mla_attention(mq, kv, kpe, l, batch, t, heads, kv_cache, cache_kpe4, tables64):
    has_cache = kv_cache is not None
    rope = tables64 is not None
    tq = _q_tile(t)
    cache_specs, extra, args = [], [], [mq, kv, kpe]
    if has_cache:
        s_cache = cache_kpe4.shape[2]
        cache_specs = [pl.BlockSpec((s_cache, kv.shape[1]), lambda b, i: (b, 0)), _cache_spec(l, s_cache, LANES, 0)]
        args += [kv_cache, cache_kpe4]
    if rope:
        extra = [pl.BlockSpec((tq, LANES), lambda b, i: (i, 0))] * 3
        args += list(tables64)
    kern = functools.partial(_mla_attn_kernel, heads=heads, has_cache=has_cache, rope=rope)
    return _attn_call(kern, "mla_attention", batch, t, heads * MLA_V, [(mq.shape[1], 0)],
                      [(kv.shape[1], 0), (LANES, 0)], cache_specs, extra, args)


def _rope_tables(t_len, width):
    half = width // 2
    quarter = half // 2
    t = jnp.arange(t_len, dtype=jnp.int32)
    rows = (t // GRID_W).astype(F32)
    cols = (t % GRID_W).astype(F32)
    lane = jnp.arange(LANES, dtype=jnp.int32) % width
    inv = ROPE_THETA ** (-(jnp.arange(0, half, 2, dtype=F32) / half))
    inv_lane = inv[(lane % half) % quarter]
    pos = jnp.where((lane < half)[None, :], rows[:, None], cols[:, None])
    ang = pos * inv_lane[None, :]
    cos, sin = jnp.cos(ang), jnp.sin(ang)
    first = ((lane % half) < quarter)[None, :]
    zero = jnp.zeros_like(sin)
    return cos, jnp.where(first, -sin, zero), jnp.where(first, zero, sin)


def _layer(x, l, w, mod4, grp, batch, t, caches, tables, emit_state, state_in=None):
    ffn = lambda h, w_in, w_out, norm, c0, name: _matmul_resid(
        [_norm_swiglu(h, norm, l, mod4, grp, c0, c0 + 1, w_in)],
        w_out, l, h, mod4, grp, c0 + 2, 0.5, name, tn_target=512)
    x = ffn(x, w["ffn1_w_in"], w["ffn1_w_out"], w["norm_ffn1"], 0, "ffn1_out")

    u = _normmod(x, w["norm_mix"], l, mod4, grp, 3, 4)
    dims = w["dims"]
    n_dqk, n_dv, n_gq, n_gk, q_lora, kv_lora = dims
    n_main = n_dqk + n_dv + n_gq + 2 * n_gk + q_lora + kv_lora
    p = _matmul_t(u, w["w_in_t"], l, F32, "mixer_in", n_main, tn_target=512)
    pk = _matmul_t_tail(u, w["w_in_t"], l, n_main, "mixer_in_kpe")
    post = _post(p, pk, l, w["ffn1_w_in"].shape[0], dims, w["gqa_q_norm"], w["gqa_k_norm"], w["mla_q_norm"],
                 w["mla_kv_norm"], tables[0] + tables[1] if tables is not None else None, t, emit_state, state_in)
    qkv, cqn, ckvn, kpe = post[:4]

    diff_heads = n_dv // DIFF_V
    q_heads, kv_heads = n_gq // HEAD_DIM, n_gk // HEAD_DIM
    mla_heads = w["mla_w_ukv"].shape[-1] // (MLA_NOPE + MLA_V)
    has_cache = caches is not None
    o_diff = _diff_attention(qkv, l, batch, t, diff_heads,
                             caches["diff_k"] if has_cache else None, caches["diff_v"] if has_cache else None,
                             w["diff_l"], w["diff_subln"])
    o_gqa = _gqa_attention(qkv, l, batch, t, n_dqk + n_dv, q_heads, kv_heads,
                           caches["gqa_k"] if has_cache else None, caches["gqa_v"] if has_cache else None)
    mq = _matmul(cqn, w["mla_w_uq"], l, F32 if tables is not None else BF16, "mla_q_up", tn_target=1024)
    kv = _matmul(ckvn, w["mla_w_ukv"], l, BF16, "mla_kv_up", tn_target=w["mla_w_ukv"].shape[-1])
    kv_cache = _matmul_cache(caches["mla_ckv"], l, w["mla_w_ukv"], l, BF16, "mla_kv_up_cache") if has_cache else None
    o_mla = _mla_attention(mq, kv, kpe, l, batch, t, mla_heads, kv_cache,
                           caches["mla_kpe"] if has_cache else None, tables[1] if tables is not None else None)
    x = _matmul_resid([o_diff, o_gqa, o_mla], w["w_out"], l, x, mod4, grp, 5, 1.0, "mixer_out", tn_target=512)

    x = ffn(x, w["ffn2_w_in"], w["ffn2_w_out"], w["norm_ffn2"], 6, "ffn2_out")
    state = (list(post[4:]), pk[:, :MLA_ROPE]) if emit_state else None
    return x, state


def kernel(x_prompt, x_sample, cache_diff_k, cache_diff_v, cache_gqa_k, cache_gqa_v, cache_mla_ckv, cache_mla_kpe, c, c_ctx, w_mod, b_mod, norm_ffn1, norm_mix, norm_ffn2, ffn1_w_in, ffn1_w_out, ffn2_w_in, ffn2_w_out, w_in, w_out, diff_lq1, diff_lk1, diff_lq2, diff_lk2, diff_subln, gqa_q_norm, gqa_k_norm, mla_q_norm, mla_kv_norm, mla_w_uq, mla_w_ukv, final_norm):
    batch, seq, d = x_prompt.shape
    dec_batch, dec_seq, _ = x_sample.shape
    depth = w_mod.shape[0]
    past = cache_diff_k.shape[2]
    diff_heads = cache_diff_k.shape[3]
    kv_heads = cache_gqa_k.shape[3]
    q_lora, kv_lora = mla_q_norm.shape[-1], mla_kv_norm.shape[-1]
    mla_heads = mla_w_ukv.shape[-1] // (MLA_NOPE + MLA_V)
    n_dqk = 2 * diff_heads * 2 * DIFF_QK
    n_dv = diff_heads * DIFF_V
    n_gk = kv_heads * HEAD_DIM
    n_gq = w_in.shape[-1] - (n_dqk + n_dv + 2 * n_gk + q_lora + kv_lora + MLA_ROPE)
    dims = (n_dqk, n_dv, n_gq, n_gk, q_lora, kv_lora)
    assert dec_batch + 1 <= MOD_ROWS

    cond = jnp.zeros((MOD_ROWS, d), F32).at[0].set(c_ctx).at[1:1 + dec_batch].set(c)
    mod4 = _modulation(cond, w_mod, b_mod).reshape(depth, MOD_ROWS, 1, N_MOD * d)

    uq = mla_w_uq.reshape(depth, q_lora, mla_heads, MLA_NOPE + MLA_ROPE)
    uq_pe = jnp.pad(uq[..., MLA_NOPE:], ((0, 0), (0, 0), (0, 0), (0, LANES - MLA_ROPE)))
    uq = jnp.concatenate([uq[..., :MLA_NOPE].reshape(depth, q_lora, -1), uq_pe.reshape(depth, q_lora, -1)], axis=-1)
    cache_kpe = jnp.pad(cache_mla_kpe, ((0, 0), (0, 0), (0, 0), (0, LANES - MLA_ROPE)))
    w_in_t = jnp.swapaxes(w_in, 1, 2)

    r3 = lambda a: a.reshape(depth, 1, a.shape[-1])
    weights = {
        "ffn1_w_in": ffn1_w_in, "ffn1_w_out": ffn1_w_out, "ffn2_w_in": ffn2_w_in, "ffn2_w_out": ffn2_w_out,
        "w_in_t": w_in_t, "w_out": w_out, "mla_w_uq": uq, "mla_w_ukv": mla_w_ukv,
        "norm_ffn1": r3(norm_ffn1), "norm_mix": r3(norm_mix), "norm_ffn2": r3(norm_ffn2),
        "gqa_q_norm": r3(gqa_q_norm), "gqa_k_norm": r3(gqa_k_norm),
        "mla_q_norm": r3(mla_q_norm), "mla_kv_norm": r3(mla_kv_norm),
        "diff_l": (r3(diff_lq1), r3(diff_lk1), r3(diff_lq2), r3(diff_lk2)), "diff_subln": r3(diff_subln),
        "dims": dims,
    }
    caches = {
        "diff_k": cache_diff_k.reshape(dec_batch, depth, past, -1),
        "diff_v": cache_diff_v.reshape(dec_batch, depth, past, -1),
        "gqa_k": cache_gqa_k.reshape(dec_batch, depth, past, -1),
        "gqa_v": cache_gqa_v.reshape(dec_batch, depth, past, -1),
        "mla_ckv": cache_mla_ckv, "mla_kpe": cache_kpe,
    }
    tables = (_rope_tables(dec_seq, HEAD_DIM), _rope_tables(dec_seq, DIFF_QK))

    h = x_prompt.reshape(batch * seq, d)
    state, kpes = None, []
    for l in range(depth):
        h, (state, kpe_l) = _layer(h, l, weights, mod4, (0, batch * seq), batch, seq, None, None, True, state)
        kpes.append(kpe_l.reshape(batch, seq, MLA_ROPE))
    y_prompt = _final_norm(h, final_norm).reshape(batch, seq, d)

    h = x_sample.reshape(dec_batch * dec_seq, d)
    for l in range(depth):
        h, _ = _layer(h, l, weights, mod4, (1, dec_seq), dec_batch, dec_seq, caches, tables, False)
    y_sample = _final_norm(h, final_norm).reshape(dec_batch, dec_seq, d)

    dk, dv, gk, gv, ckv = state
    lead = (batch, depth, seq)
    return (y_prompt, y_sample,
            dk.reshape(lead + (diff_heads, 2, DIFF_QK)), dv.reshape(lead + (diff_heads, DIFF_V)),
            gk.reshape(lead + (kv_heads, HEAD_DIM)), gv.reshape(lead + (kv_heads, HEAD_DIM)),
            ckv, jnp.stack(kpes, axis=1))
```

```python
import functools
import math
import operator

import jax
import jax.numpy as jnp
from jax import lax
from jax.experimental import pallas as pl
from jax.experimental.pallas import tpu as pltpu

F32 = jnp.float32
BF16 = jnp.bfloat16

NORM_EPS = 1e-6
ROPE_THETA = 10000.0
GRID_W = 64
N_MOD = 9
DIFF_QK = 64
DIFF_V = 128
HEAD_DIM = 128
MLA_NOPE = 128
MLA_ROPE = 64
MLA_V = 128
LANES = 128
MOD_ROWS = 16
VMEM_LIMIT_BYTES = 56 * 1024 * 1024
W_TILE_BYTES_MAX = 6 * 1024 * 1024
LOG2_E = 1.4426950408889634


def _params(sem):
    return pltpu.CompilerParams(dimension_semantics=sem, vmem_limit_bytes=VMEM_LIMIT_BYTES)


def _silu(x):
    return x / (1.0 + jnp.exp(-x))


def _rms(x, gain):
    y = x * lax.rsqrt(jnp.mean(x * x, axis=-1, keepdims=True) + NORM_EPS)
    return y * gain


def _rope(x, cos, sin_lo, sin_hi, shift):
    return (x * cos + pltpu.roll(x, LANES - shift, 1) * sin_lo + pltpu.roll(x, shift, 1) * sin_hi)


_NT = (((1,), (1,)), ((), ()))


def _dot_nt(a, b):
    return lax.dot_general(a, b, _NT, preferred_element_type=F32)


def _sum(xs):
    return functools.reduce(operator.add, xs)


def _mod_kernel(c_ref, w_ref, b_ref, o_ref):
    a = _silu(c_ref[...]).astype(BF16)
    o_ref[...] = jnp.dot(a, w_ref[...].astype(BF16), preferred_element_type=F32) + b_ref[...]


def _modulation(cond, w_mod, b_mod):
    depth, d, n = w_mod.shape
    tn = 512 if n % 512 == 0 else n
    return pl.pallas_call(
        _mod_kernel,
        grid=(depth, n // tn),
        in_specs=[
            pl.BlockSpec((MOD_ROWS, d), lambda l, j: (0, 0)),
            pl.BlockSpec((None, d, tn), lambda l, j: (l, 0, j)),
            pl.BlockSpec((None, 1, tn), lambda l, j: (l, 0, j)),
        ],
        out_specs=pl.BlockSpec((None, MOD_ROWS, tn), lambda l, j: (l, 0, j)),
        out_shape=jax.ShapeDtypeStruct((depth, MOD_ROWS, n), F32),
        compiler_params=_params(("arbitrary", "arbitrary")),
        name="modulation",
    )(cond, w_mod, b_mod.reshape(depth, 1, n))


NORM_ROWS = 16


def _norm_rows(x_ref, o_ref, gain, shift):
    def body(r, carry):
        rows = pl.ds(pl.multiple_of(r * NORM_ROWS, NORM_ROWS), NORM_ROWS)
        x = x_ref[rows, :]
        y = x * lax.rsqrt(jnp.mean(x * x, axis=-1, keepdims=True) + NORM_EPS) * gain
        o_ref[rows, :] = (y if shift is None else y + shift).astype(o_ref.dtype)
        return carry

    lax.fori_loop(0, x_ref.shape[0] // NORM_ROWS, body, 0, unroll=8)


def _normmod_kernel(x_ref, g_ref, sh_ref, sc_ref, o_ref):
    _norm_rows(x_ref, o_ref, g_ref[...] * (1.0 + sc_ref[...]), sh_ref[...])


def _norm_kernel(x_ref, g_ref, o_ref):
    _norm_rows(x_ref, o_ref, g_ref[...], None)


def _row_tile(m, target):
    t = min(m, target)
    assert m % t == 0
    return t


def _normmod(x, gain3, l, mod4, grp, chunk_shift, chunk_scale, rows=None):
    m, d = x.shape
    m = m if rows is None else rows
    g_base, g_rows = grp
    tr = _row_tile(min(g_rows, m), 512)
    return pl.pallas_call(
        _normmod_kernel,
        grid=(m // tr,),
        in_specs=[
            pl.BlockSpec((tr, d), lambda i: (i, 0)),
            pl.BlockSpec((None, 1, d), lambda i: (l, 0, 0)),
            pl.BlockSpec((None, None, 1, d), lambda i: (l, g_base + (i * tr) // g_rows, 0, chunk_shift)),
            pl.BlockSpec((None, None, 1, d), lambda i: (l, g_base + (i * tr) // g_rows, 0, chunk_scale)),
        ],
        out_specs=pl.BlockSpec((tr, d), lambda i: (i, 0)),
        out_shape=jax.ShapeDtypeStruct((m, d), BF16),
        compiler_params=_params(("arbitrary",)),
        name="normmod",
    )(x, gain3, mod4, mod4)


def _final_norm(x, gain):
    m, d = x.shape
    tr = _row_tile(m, 512)
    return pl.pallas_call(
        _norm_kernel,
        grid=(m // tr,),
        in_specs=[pl.BlockSpec((tr, d), lambda i: (i, 0)), pl.BlockSpec((1, d), lambda i: (0, 0))],
        out_specs=pl.BlockSpec((tr, d), lambda i: (i, 0)),
        out_shape=jax.ShapeDtypeStruct((m, d), F32),
        compiler_params=_params(("arbitrary",)),
        name="final_norm",
    )(x, gain.reshape(1, d))


def _mm_kernel(a_ref, w_ref, o_ref):
    a = a_ref[...].astype(BF16)
    o_ref[...] = jnp.dot(a, w_ref[...].astype(BF16), preferred_element_type=F32).astype(o_ref.dtype)


def _mm_t_kernel(a_ref, wt_ref, o_ref):
    o_ref[...] = _dot_nt(a_ref[...], wt_ref[...].astype(BF16)).astype(o_ref.dtype)


def _mm_t_tail_kernel(a_ref, wt_ref, o_ref, *, valid):
    wt = wt_ref[...]
    row = lax.broadcasted_iota(jnp.int32, wt.shape, 0)
    wt = jnp.where(row < valid, wt, 0.0).astype(BF16)
    o_ref[...] = _dot_nt(a_ref[...], wt)


def _swiglu_and_norm_next(a_ref, nxt_ref, rows, xc_ref, gain, shift, wg_ref, wu_ref, o_ref):
    x = xc_ref[...]
    y = x * lax.rsqrt(jnp.mean(x * x, axis=-1, keepdims=True) + NORM_EPS) * gain + shift
    nxt_ref[rows, :] = y.astype(BF16)
    a = a_ref[...]
    g = jnp.dot(a, wg_ref[...].astype(BF16), preferred_element_type=F32)
    u = jnp.dot(a, wu_ref[...].astype(BF16), preferred_element_type=F32)
    o_ref[...] = (_silu(g) * u).astype(o_ref.dtype)


def _mm_swiglu_fused_kernel(a0_ref, xc_ref, g_ref, sh_ref, sc_ref, wg_ref, wu_ref, o_ref, buf0, buf1, *,
                            n_chunks, chunk):
    i, j = pl.program_id(0), pl.program_id(1)

    @pl.when((i == 0) & (j == 0))
    def _():
        buf0[...] = a0_ref[...]

    gain = g_ref[...] * (1.0 + sc_ref[...])
    rows = pl.ds(pl.multiple_of(jnp.minimum(j, n_chunks - 1) * chunk, chunk), chunk)
    args = (rows, xc_ref, gain, sh_ref[...], wg_ref, wu_ref, o_ref)

    @pl.when(i % 2 == 0)
    def _():
        _swiglu_and_norm_next(buf0, buf1, *args)

    @pl.when(i % 2 == 1)
    def _():
        _swiglu_and_norm_next(buf1, buf0, *args)


def _mm_resid_kernel(*refs, widths, coef):
    n_a = len(widths)
    a_refs = refs[:n_a]
    w_ref, x_ref, gate_ref, o_ref = refs[n_a:]
    row = 0
    parts = []
    for a_ref, width in zip(a_refs, widths):
        parts.append(jnp.dot(a_ref[...], w_ref[row:row + width, :].astype(BF16), preferred_element_type=F32))
        row += width
    o_ref[...] = x_ref[...] + (coef * gate_ref[...]) * _sum(parts)


def _mm_resid_split_kernel(a_hbm, w_ref, x_ref, gate_ref, o_ref, acc_ref, head_buf, rest_buf, sems, *,
                           coef, nk, tk, tm, depth_k):
    i, j, k = pl.program_id(0), pl.program_id(1), pl.program_id(2)
    n_i, n_j = pl.num_programs(0), pl.num_programs(1)
    width = lambda kk: min(tk, depth_k - kk * tk)

    def chunk_copy(tile, kk):
        src = a_hbm.at[pl.ds(tile * tm, tm), pl.ds(kk * tk, width(kk))]
        dst = head_buf if kk == 0 else rest_buf.at[:, pl.ds((kk - 1) * tk, width(kk))]
        return pltpu.make_async_copy(src, dst, sems.at[kk])

    @pl.when((j == 0) & (k == 0))
    def _():
        @pl.when(i == 0)
        def _():
            chunk_copy(0, 0).start()

        for kk in range(1, nk):
            chunk_copy(i, kk).start()
        chunk_copy(i, 0).wait()

    for kk in range(1, nk):
        @pl.when((j == 0) & (k == kk))
        def _(kk=kk):
            chunk_copy(i, kk).wait()

    @pl.when((j == n_j - 1) & (k == 1) & (i + 1 < n_i))
    def _():
        chunk_copy(i + 1, 0).start()

    for kk in range(nk):
        @pl.when(k == kk)
        def _(kk=kk):
            rows = width(kk)
            a = head_buf[...] if kk == 0 else rest_buf[:, (kk - 1) * tk:(kk - 1) * tk + rows]
            part = jnp.dot(a, w_ref[0:rows, :].astype(BF16), preferred_element_type=F32)
            if kk > 0:
                part = acc_ref[...] + part
            if kk == nk - 1:
                o_ref[...] = x_ref[...] + (coef * gate_ref[...]) * part
            else:
                acc_ref[...] = part


def _col_tile(n, target=256):
    return target if n % target == 0 else n


def _matmul_t_tail(a, wt, l, row0, name):
    m, k = a.shape
    valid = wt.shape[1] - row0
    assert row0 % LANES == 0 and 0 < valid <= LANES
    tm = _row_tile(m, 1024)
    return pl.pallas_call(
        functools.partial(_mm_t_tail_kernel, valid=valid),
        grid=(m // tm,),
        in_specs=[
            pl.BlockSpec((tm, k), lambda i: (i, 0)),
            pl.BlockSpec((None, LANES, k), lambda i: (l, row0 // LANES, 0)),
        ],
        out_specs=pl.BlockSpec((tm, LANES), lambda i: (i, 0)),
        out_shape=jax.ShapeDtypeStruct((m, LANES), F32),
        compiler_params=_params(("arbitrary",)),
        name=name,
    )(a, wt)


def _matmul_t(a, wt, l, out_dtype, name, n, tn_target=256):
    m, k = a.shape
    tm = _row_tile(m, 1024)
    tn = _col_tile(n, tn_target)
    return pl.pallas_call(
        _mm_t_kernel,
        grid=(m // tm, n // tn),
        in_specs=[
            pl.BlockSpec((tm, k), lambda i, j: (i, 0)),
            pl.BlockSpec((None, tn, k), lambda i, j: (l, j, 0)),
        ],
        out_specs=pl.BlockSpec((tm, tn), lambda i, j: (i, j)),
        out_shape=jax.ShapeDtypeStruct((m, n), out_dtype),
        compiler_params=_params(("arbitrary", "arbitrary")),
        name=name,
    )(a, wt)


def _matmul(a, w, l, out_dtype, name, n=None, tn_target=256):
    m, k = a.shape
    n = w.shape[-1] if n is None else n
    tm = _row_tile(m, 1024)
    tn = _col_tile(n, tn_target)
    return pl.pallas_call(
        _mm_kernel,
        grid=(m // tm, n // tn),
        in_specs=[
            pl.BlockSpec((tm, k), lambda i, j: (i, 0)),
            pl.BlockSpec((None, k, tn), lambda i, j: (l, 0, j)),
        ],
        out_specs=pl.BlockSpec((tm, tn), lambda i, j: (i, j)),
        out_shape=jax.ShapeDtypeStruct((m, n), out_dtype),
        compiler_params=_params(("arbitrary", "arbitrary")),
        name=name,
    )(a, w)


def _matmul_cache(a4, l, w, lw, out_dtype, name):
    b, _, s, k = a4.shape
    n = w.shape[-1]
    tn = n
    return pl.pallas_call(
        _mm_kernel,
        grid=(b, n // tn),
        in_specs=[
            pl.BlockSpec((None, None, s, k), lambda i, j: (i, l, 0, 0)),
            pl.BlockSpec((None, k, tn), lambda i, j: (lw, 0, j)),
        ],
        out_specs=pl.BlockSpec((s, tn), lambda i, j: (i, j)),
        out_shape=jax.ShapeDtypeStruct((b * s, n), out_dtype),
        compiler_params=_params(("arbitrary", "arbitrary")),
        name=name,
    )(a4, w)


def _norm_swiglu(x, gain3, l, mod4, grp, chunk_shift, chunk_scale, w):
    m, k = x.shape
    f = w.shape[-1] // 2
    g_base, g_rows = grp
    tm = _row_tile(g_rows, 1024)
    tn = _col_tile(f)
    n_i, nj = m // tm, f // tn
    chunk = 16
    while chunk * nj < tm or tm % chunk:
        chunk += 16
    n_chunks = tm // chunk
    a0 = _normmod(x, gain3, l, mod4, grp, chunk_shift, chunk_scale, rows=tm)
    nxt = lambda i: jnp.minimum(i + 1, n_i - 1)
    mod_spec = lambda c: pl.BlockSpec((None, None, 1, k), lambda i, j: (l, g_base + (nxt(i) * tm) // g_rows, 0, c))
    kern = functools.partial(_mm_swiglu_fused_kernel, n_chunks=n_chunks, chunk=chunk)
    return pl.pallas_call(
        kern,
        grid=(n_i, nj),
        in_specs=[
            pl.BlockSpec((tm, k), lambda i, j: (0, 0), pipeline_mode=pl.Buffered(1)),
            pl.BlockSpec((chunk, k), lambda i, j: (nxt(i) * n_chunks + jnp.minimum(j, n_chunks - 1), 0)),
            pl.BlockSpec((None, 1, k), lambda i, j: (l, 0, 0)),
            mod_spec(chunk_shift),
            mod_spec(chunk_scale),
            pl.BlockSpec((None, k, tn), lambda i, j: (l, 0, j)),
            pl.BlockSpec((None, k, tn), lambda i, j: (l, 0, j + nj)),
        ],
        out_specs=pl.BlockSpec((tm, tn), lambda i, j: (i, j)),
        out_shape=jax.ShapeDtypeStruct((m, f), BF16),
        scratch_shapes=[pltpu.VMEM((tm, k), BF16)] * 2,
        compiler_params=_params(("arbitrary", "arbitrary")),
        name="ffn_in",
    )(a0, x, gain3, mod4, mod4, w, w)


def _matmul_resid(a_parts, w, l, x, mod4, grp, chunk, coef, name, tn_target=256):
    m = a_parts[0].shape[0]
    widths = tuple(a.shape[1] for a in a_parts)
    k = sum(widths)
    n = w.shape[-1]
    g_base, g_rows = grp
    tm = _row_tile(g_rows, 1024)
    tn = _col_tile(n, tn_target)
    nk, tk = 1, k
    while len(a_parts) == 1 and tk * tn * 4 > W_TILE_BYTES_MAX:
        nk += 1
        tk = pl.cdiv(pl.cdiv(k, nk), LANES) * LANES
    assert (nk - 1) * tk < k
    per_chunk = n // tn
    common_specs = [
        pl.BlockSpec((None, tk, tn), lambda i, j, kk: (l, kk, j)),
        pl.BlockSpec((tm, tn), lambda i, j, kk: (i, j)),
        pl.BlockSpec((None, None, 1, tn), lambda i, j, kk: (l, g_base + (i * tm) // g_rows, 0, chunk * per_chunk + j)),
    ]
    if nk == 1:
        kern = functools.partial(_mm_resid_kernel, widths=widths, coef=coef)
        a_specs = [pl.BlockSpec((tm, width), lambda i, j, kk: (i, 0)) for width in widths]
        scratch = []
    else:
        kern = functools.partial(_mm_resid_split_kernel, coef=coef, nk=nk, tk=tk, tm=tm, depth_k=k)
        a_specs = [pl.BlockSpec(memory_space=pl.ANY)]
        scratch = [pltpu.VMEM((tm, tn), F32), pltpu.VMEM((tm, tk), BF16), pltpu.VMEM((tm, k - tk), BF16),
                   pltpu.SemaphoreType.DMA((nk,))]
    return pl.pallas_call(
        kern,
        grid=(m // tm, n // tn, nk),
        in_specs=a_specs + common_specs,
        out_specs=pl.BlockSpec((tm, tn), lambda i, j, kk: (i, j)),
        out_shape=jax.ShapeDtypeStruct((m, n), F32),
        scratch_shapes=scratch,
        compiler_params=_params(("arbitrary", "arbitrary", "arbitrary")),
        name=name,
    )(*a_parts, w, x, mod4)


def _post_kernel(*refs, dims, rope, emit_state, n_alias):
    n_dqk, n_dv, n_gq, n_gk, q_lora, kv_lora = dims
    p_ref, pk_ref, gqn_ref, gkn_ref, mqn_ref, mkvn_ref = refs[:6]
    pos = 6
    if rope:
        c128, lo128, hi128, c64, lo64, hi64 = (r[...] for r in refs[pos:pos + 6])
        pos += 6
    pos += n_alias
    qkv_ref, cq_ref, ckv_ref, kpe_ref = refs[pos:pos + 4]
    pos += 4
    if emit_state:
        dk_f_ref, dv_f_ref, gk_f_ref, gv_f_ref, ckv_f_ref = refs[pos:pos + 5]

    col = 0
    for _ in range(n_dqk // LANES):
        x = p_ref[:, col:col + LANES]
        if rope:
            x = _rope(x, c64, lo64, hi64, DIFF_QK // 4)
        qkv_ref[:, col:col + LANES] = x.astype(BF16)
        col += LANES
    if emit_state:
        dk_f_ref[...] = p_ref[:, n_dqk // 2:n_dqk]
    dv = p_ref[:, col:col + n_dv]
    qkv_ref[:, col:col + n_dv] = dv.astype(BF16)
    if emit_state:
        dv_f_ref[...] = dv
    col += n_dv
    for _ in range(n_gq // LANES):
        x = _rms(p_ref[:, col:col + LANES], gqn_ref[...])
        if rope:
            x = _rope(x, c128, lo128, hi128, HEAD_DIM // 4)
        qkv_ref[:, col:col + LANES] = x.astype(BF16)
        col += LANES
    for h in range(n_gk // LANES):
        x = _rms(p_ref[:, col:col + LANES], gkn_ref[...])
        if emit_state:
            gk_f_ref[:, h * LANES:(h + 1) * LANES] = x
        if rope:
            x = _rope(x, c128, lo128, hi128, HEAD_DIM // 4)
        qkv_ref[:, col:col + LANES] = x.astype(BF16)
        col += LANES
    gv = p_ref[:, col:col + n_gk]
    qkv_ref[:, col:col + n_gk] = gv.astype(BF16)
    if emit_state:
        gv_f_ref[...] = gv
    col += n_gk
    cq_ref[...] = _rms(p_ref[:, col:col + q_lora], mqn_ref[...]).astype(BF16)
    col += q_lora
    ckv = _rms(p_ref[:, col:col + kv_lora], mkvn_ref[...])
    ckv_ref[...] = ckv.astype(BF16)
    if emit_state:
        ckv_f_ref[...] = ckv
    kpe = pk_ref[...]
    if rope:
        kpe = _rope(kpe, c64, lo64, hi64, MLA_ROPE // 4)
    kpe_ref[...] = kpe.astype(BF16)


def _post(p, pk, l, depth, dims, gqn, gkn, mqn, mkvn, tables, t_len, emit_state, state_in):
    m, n = p.shape
    n_dqk, n_dv, n_gq, n_gk, q_lora, kv_lora = dims
    n_qkv = n_dqk + n_dv + n_gq + 2 * n_gk
    rope = tables is not None
    tr = _row_tile(t_len, 256)
    nt = t_len // tr
    row_spec = lambda width: pl.BlockSpec((tr, width), lambda i: (i, 0))
    gain_spec = lambda width: pl.BlockSpec((None, 1, width), lambda i: (l, 0, 0))
    in_specs = [row_spec(n), row_spec(LANES), gain_spec(HEAD_DIM), gain_spec(HEAD_DIM), gain_spec(q_lora),
                gain_spec(kv_lora)]
    args = [p, pk, gqn, gkn, mqn, mkvn]
    if rope:
        in_specs += [pl.BlockSpec((tr, LANES), lambda i: (i % nt, 0))] * 6
        args += list(tables)
    out_widths = [(n_qkv, BF16), (q_lora, BF16), (kv_lora, BF16), (LANES, BF16)]
    out_specs = [row_spec(width) for width, _ in out_widths]
    out_shape = [jax.ShapeDtypeStruct((m, width), dt) for width, dt in out_widths]
    aliases = {}
    n_alias = 0
    if emit_state:
        state_widths = [n_dqk // 2, n_dv, n_gk, n_gk, kv_lora]
        if state_in is not None:
            n_alias = len(state_in)
            aliases = {len(args) + s: len(out_shape) + s for s in range(n_alias)}
            in_specs += [pl.BlockSpec(memory_space=pl.ANY)] * n_alias
            args += list(state_in)
        out_specs += [pl.BlockSpec((None, None, tr, width), lambda i: (i // nt, l, i % nt, 0)) for width in state_widths]
        out_shape += [jax.ShapeDtypeStruct((m // t_len, depth, t_len, width), F32) for width in state_widths]
    kern = functools.partial(_post_kernel, dims=dims, rope=rope, emit_state=emit_state, n_alias=n_alias)
    return pl.pallas_call(
        kern,
        grid=(m // tr,),
        in_specs=in_specs,
        out_specs=out_specs,
        out_shape=out_shape,
        input_output_aliases=aliases,
        compiler_params=_params(("arbitrary",)),
        name="mixer_post",
    )(*args)


def _attend(q_parts, segments, c):
    s = [_sum([_dot_nt(q, k) for q, k in zip(q_parts, ks)]) for ks, _ in segments]
    m = functools.reduce(jnp.maximum, [jnp.max(x, axis=-1, keepdims=True) for x in s])
    e = [jnp.exp2((x - m) * c) for x in s]
    den = _sum([jnp.sum(x, axis=-1, keepdims=True) for x in e])
    o = _sum([jnp.dot(x.astype(BF16), v, preferred_element_type=F32) for x, (_, v) in zip(e, segments)])
    return o / den


def _diff_attn_kernel(*refs, heads, has_cache, lam_init):
    q_ref, k_ref, v_ref = refs[:3]
    pos = 3
    if has_cache:
        kc_ref, vc_ref = refs[pos:pos + 2]
        pos += 2
    lq1, lk1, lq2, lk2, sub_ref, o_ref = refs[pos:pos + 6]
    lam = (jnp.exp(jnp.sum(lq1[...] * lk1[...], axis=-1, keepdims=True))
           - jnp.exp(jnp.sum(lq2[...] * lk2[...], axis=-1, keepdims=True)) + lam_init)
    c = DIFF_QK ** -0.5 * LOG2_E
    lane = lax.broadcasted_iota(jnp.int32, (q_ref.shape[0], LANES), 1)
    for h in range(heads):
        cols = slice(h * LANES, (h + 1) * LANES)
        q = q_ref[:, cols]
        zero = jnp.zeros_like(q)
        segments = [([k_ref[:, cols]], v_ref[:, cols])]
        if has_cache:
            segments.insert(0, ([kc_ref[:, cols].astype(BF16)], vc_ref[:, cols].astype(BF16)))
        o0 = _attend([jnp.where(lane < DIFF_QK, q, zero)], segments, c)
        o1 = _attend([jnp.where(lane >= DIFF_QK, q, zero)], segments, c)
        o = o0 - lam * o1
        o_ref[:, cols] = (_rms(o, sub_ref[...]) * (1.0 - lam_init)).astype(o_ref.dtype)


def _gqa_attn_kernel(*refs, kv_heads, group, has_cache):
    q_ref, k_ref, v_ref = refs[:3]
    pos = 3
    if has_cache:
        kc_ref, vc_ref = refs[pos:pos + 2]
        pos += 2
    o_ref = refs[pos]
    c = HEAD_DIM ** -0.5 * LOG2_E
    for kvh in range(kv_heads):
        cols = slice(kvh * LANES, (kvh + 1) * LANES)
        segments = [([k_ref[:, cols]], v_ref[:, cols])]
        if has_cache:
            segments.insert(0, ([kc_ref[:, cols].astype(BF16)], vc_ref[:, cols].astype(BF16)))
        for g in range(group):
            qcols = slice((kvh * group + g) * LANES, (kvh * group + g + 1) * LANES)
            o_ref[:, qcols] = _attend([q_ref[:, qcols]], segments, c).astype(o_ref.dtype)


def _mla_attn_kernel(*refs, heads, has_cache, rope):
    mq_ref, kv_ref, kp_ref = refs[:3]
    pos = 3
    if has_cache:
        kvc_ref, kpc_ref = refs[pos:pos + 2]
        pos += 2
    if rope:
        c64, lo64, hi64 = (r[...] for r in refs[pos:pos + 3])
        pos += 3
    o_ref = refs[pos]
    c = (MLA_NOPE + MLA_ROPE) ** -0.5 * LOG2_E
    kp = kp_ref[...]
    if has_cache:
        kpc = kpc_ref[...].astype(BF16)
    for h in range(heads):
        cols = slice(h * LANES, (h + 1) * LANES)
        kcols = slice(2 * h * LANES, (2 * h + 1) * LANES)
        vcols = slice((2 * h + 1) * LANES, (2 * h + 2) * LANES)
        qn = mq_ref[:, cols].astype(BF16)
        qp = mq_ref[:, (heads + h) * LANES:(heads + h + 1) * LANES]
        if rope:
            qp = _rope(qp, c64, lo64, hi64, MLA_ROPE // 4)
        q = jnp.concatenate([qn, qp.astype(BF16)], axis=-1)
        segments = [([jnp.concatenate([kv_ref[:, kcols], kp], axis=-1)], kv_ref[:, vcols])]
        if has_cache:
            segments.insert(0, ([jnp.concatenate([kvc_ref[:, kcols], kpc], axis=-1)], kvc_ref[:, vcols]))
        o_ref[:, cols] = _attend([q], segments, c).astype(o_ref.dtype)


def _q_tile(t, target=256):
    return target if t % target == 0 else t


def _attn_call(kern, name, batch, t, out_width, q_specs, kv_specs, cache_specs, extra_specs, args, tq_target=256,
               max_sub=4):
    tq = _q_tile(t, tq_target)
    nq = t // tq
    n_sub = 1
    if nq == 1 and not cache_specs:
        n_sub = max(s for s in (1, 2, 4) if batch % s == 0 and s <= max_sub)
    if n_sub > 1:
        n_rows = len(q_specs) + len(kv_specs)
        inner = kern

        def kern(*refs):
            for s in range(n_sub):
                view = lambda r: r.at[pl.ds(s * t, t), :]
                inner(*[view(r) for r in refs[:n_rows]], *refs[n_rows:-1], view(refs[-1]))

    in_specs = ([pl.BlockSpec((n_sub * tq, w), lambda b, i, c=c: (b * nq + i, c)) for w, c in q_specs]
                + [pl.BlockSpec((n_sub * t, w), lambda b, i, c=c: (b, c)) for w, c in kv_specs]
                + cache_specs + extra_specs)
    return pl.pallas_call(
        kern,
        grid=(batch // n_sub, nq),
        in_specs=in_specs,
        out_specs=pl.BlockSpec((n_sub * tq, out_width), lambda b, i: (b * nq + i, 0)),
        out_shape=jax.ShapeDtypeStruct((batch * t, out_width), BF16),
        compiler_params=_params(("arbitrary", "arbitrary")),
        name=name,
    )(*args)


def _cache_spec(l, s_cache, width, col):
    return pl.BlockSpec((None, None, s_cache, width), lambda b, i: (b, l, 0, col))


def _diff_attention(qkv, l, batch, t, heads, cache_k4, cache_v4, lparams, subln):
    has_cache = cache_k4 is not None
    w = heads * LANES
    cache_specs, args = [], [qkv, qkv, qkv]
    if has_cache:
        cache_specs = [_cache_spec(l, cache_k4.shape[2], w, 0)] * 2
        args += [cache_k4, cache_v4]
    extra = [pl.BlockSpec((None, 1, DIFF_QK), lambda b, i: (l, 0, 0))] * 4
    extra += [pl.BlockSpec((None, 1, DIFF_V), lambda b, i: (l, 0, 0))]
    args += list(lparams) + [subln]
    lam_init = 0.8 - 0.6 * math.exp(-0.3 * l)
    kern = functools.partial(_diff_attn_kernel, heads=heads, has_cache=has_cache, lam_init=lam_init)
    return _attn_call(kern, "diff_attention", batch, t, w, [(w, 0)], [(w, 1), (w, 2)], cache_specs, extra, args,
                      tq_target=512, max_sub=2)


def _gqa_attention(qkv, l, batch, t, col0, q_heads, kv_heads, cache_k4, cache_v4):
    has_cache = cache_k4 is not None
    qw, kw = q_heads * HEAD_DIM, kv_heads * HEAD_DIM
    assert col0 % qw == 0 and (col0 + qw) % kw == 0
    k_blk = (col0 + qw) // kw
    cache_specs, args = [], [qkv, qkv, qkv]
    if has_cache:
        cache_specs = [_cache_spec(l, cache_k4.shape[2], kw, 0)] * 2
        args += [cache_k4, cache_v4]
    kern = functools.partial(_gqa_attn_kernel, kv_heads=kv_heads, group=q_heads // kv_heads, has_cache=has_cache)
    return _attn_call(kern, "gqa_attention", batch, t, qw, [(qw, col0 // qw)], [(kw, k_blk), (kw, k_blk + 1)],
                      cache_specs, [], args, tq_target=512)


def _mla_attention(mq, kv, kpe, l, batch, t, heads, kv_cache, cache_kpe4, tables64):
    has_cache = kv_cache is not None
    rope = tables64 is not None
    tq = _q_tile(t)
    cache_specs, extra, args = [], [], [mq, kv, kpe]
    if has_cache:
        s_cache = cache_kpe4.shape[2]
        cache_specs = [pl.BlockSpec((s_cache, kv.shape[1]), lambda b, i: (b, 0)), _cache_spec(l, s_cache, LANES, 0)]
        args += [kv_cache, cache_kpe4]
    if rope:
        extra = [pl.BlockSpec((tq, LANES), lambda b, i: (i, 0))] * 3
        args += list(tables64)
    kern = functools.partial(_mla_attn_kernel, heads=heads, has_cache=has_cache, rope=rope)
    return _attn_call(kern, "mla_attention", batch, t, heads * MLA_V, [(mq.shape[1], 0)],
                      [(kv.shape[1], 0), (LANES, 0)], cache_specs, extra, args)


def _rope_tables(t_len, width):
    half = width // 2
    quarter = half // 2
    t = jnp.arange(t_len, dtype=jnp.int32)
    rows = (t // GRID_W).astype(F32)
    cols = (t % GRID_W).astype(F32)
    lane = jnp.arange(LANES, dtype=jnp.int32) % width
    inv = ROPE_THETA ** (-(jnp.arange(0, half, 2, dtype=F32) / half))
    inv_lane = inv[(lane % half) % quarter]
    pos = jnp.where((lane < half)[None, :], rows[:, None], cols[:, None])
    ang = pos * inv_lane[None, :]
    cos, sin = jnp.cos(ang), jnp.sin(ang)
    first = ((lane % half) < quarter)[None, :]
    zero = jnp.zeros_like(sin)
    return cos, jnp.where(first, -sin, zero), jnp.where(first, zero, sin)


def _layer(x, l, w, mod4, grp, batch, t, caches, tables, emit_state, state_in=None):
    ffn = lambda h, w_in, w_out, norm, c0, name: _matmul_resid(
        [_norm_swiglu(h, norm, l, mod4, grp, c0, c0 + 1, w_in)],
        w_out, l, h, mod4, grp, c0 + 2, 0.5, name, tn_target=512)
    x = ffn(x, w["ffn1_w_in"], w["ffn1_w_out"], w["norm_ffn1"], 0, "ffn1_out")

    u = _normmod(x, w["norm_mix"], l, mod4, grp, 3, 4)
    dims = w["dims"]
    n_dqk, n_dv, n_gq, n_gk, q_lora, kv_lora = dims
    n_main = n_dqk + n_dv + n_gq + 2 * n_gk + q_lora + kv_lora
    p = _matmul_t(u, w["w_in_t"], l, F32, "mixer_in", n_main, tn_target=512)
    pk = _matmul_t_tail(u, w["w_in_t"], l, n_main, "mixer_in_kpe")
    post = _post(p, pk, l, w["ffn1_w_in"].shape[0], dims, w["gqa_q_norm"], w["gqa_k_norm"], w["mla_q_norm"],
                 w["mla_kv_norm"], tables[0] + tables[1] if tables is not None else None, t, emit_state, state_in)
    qkv, cqn, ckvn, kpe = post[:4]

    diff_heads = n_dv // DIFF_V
    q_heads, kv_heads = n_gq // HEAD_DIM, n_gk // HEAD_DIM
    mla_heads = w["mla_w_ukv"].shape[-1] // (MLA_NOPE + MLA_V)
    has_cache = caches is not None
    o_diff = _diff_attention(qkv, l, batch, t, diff_heads,
                             caches["diff_k"] if has_cache else None, caches["diff_v"] if has_cache else None,
                             w["diff_l"], w["diff_subln"])
    o_gqa = _gqa_attention(qkv, l, batch, t, n_dqk + n_dv, q_heads, kv_heads,
                           caches["gqa_k"] if has_cache else None, caches["gqa_v"] if has_cache else None)
    mq = _matmul(cqn, w["mla_w_uq"], l, F32 if tables is not None else BF16, "mla_q_up", tn_target=1024)
    kv = _matmul(ckvn, w["mla_w_ukv"], l, BF16, "mla_kv_up", tn_target=w["mla_w_ukv"].shape[-1])
    kv_cache = _matmul_cache(caches["mla_ckv"], l, w["mla_w_ukv"], l, BF16, "mla_kv_up_cache") if has_cache else None
    o_mla = _mla_attention(mq, kv, kpe, l, batch, t, mla_heads, kv_cache,
                           caches["mla_kpe"] if has_cache else None, tables[1] if tables is not None else None)
    x = _matmul_resid([o_diff, o_gqa, o_mla], w["w_out"], l, x, mod4, grp, 5, 1.0, "mixer_out", tn_target=512)

    x = ffn(x, w["ffn2_w_in"], w["ffn2_w_out"], w["norm_ffn2"], 6, "ffn2_out")
    state = (list(post[4:]), pk[:, :MLA_ROPE]) if emit_state else None
    return x, state


def kernel(x_prompt, x_sample, cache_diff_k, cache_diff_v, cache_gqa_k, cache_gqa_v, cache_mla_ckv, cache_mla_kpe, c, c_ctx, w_mod, b_mod, norm_ffn1, norm_mix, norm_ffn2, ffn1_w_in, ffn1_w_out, ffn2_w_in, ffn2_w_out, w_in, w_out, diff_lq1, diff_lk1, diff_lq2, diff_lk2, diff_subln, gqa_q_norm, gqa_k_norm, mla_q_norm, mla_kv_norm, mla_w_uq, mla_w_ukv, final_norm):
    batch, seq, d = x_prompt.shape
    dec_batch, dec_seq, _ = x_sample.shape
    depth = w_mod.shape[0]
    past = cache_diff_k.shape[2]
    diff_heads = cache_diff_k.shape[3]
    kv_heads = cache_gqa_k.shape[3]
    q_lora, kv_lora = mla_q_norm.shape[-1], mla_kv_norm.shape[-1]
    mla_heads = mla_w_ukv.shape[-1] // (MLA_NOPE + MLA_V)
    n_dqk = 2 * diff_heads * 2 * DIFF_QK
    n_dv = diff_heads * DIFF_V
    n_gk = kv_heads * HEAD_DIM
    n_gq = w_in.shape[-1] - (n_dqk + n_dv + 2 * n_gk + q_lora + kv_lora + MLA_ROPE)
    dims = (n_dqk, n_dv, n_gq, n_gk, q_lora, kv_lora)
    assert dec_batch + 1 <= MOD_ROWS

    cond = jnp.zeros((MOD_ROWS, d), F32).at[0].set(c_ctx).at[1:1 + dec_batch].set(c)
    mod4 = _modulation(cond, w_mod, b_mod).reshape(depth, MOD_ROWS, 1, N_MOD * d)

    uq = mla_w_uq.reshape(depth, q_lora, mla_heads, MLA_NOPE + MLA_ROPE)
    uq_pe = jnp.pad(uq[..., MLA_NOPE:], ((0, 0), (0, 0), (0, 0), (0, LANES - MLA_ROPE)))
    uq = jnp.concatenate([uq[..., :MLA_NOPE].reshape(depth, q_lora, -1), uq_pe.reshape(depth, q_lora, -1)], axis=-1)
    cache_kpe = jnp.pad(cache_mla_kpe, ((0, 0), (0, 0), (0, 0), (0, LANES - MLA_ROPE)))
    w_in_t = jnp.swapaxes(w_in, 1, 2)

    r3 = lambda a: a.reshape(depth, 1, a.shape[-1])
    weights = {
        "ffn1_w_in": ffn1_w_in, "ffn1_w_out": ffn1_w_out, "ffn2_w_in": ffn2_w_in, "ffn2_w_out": ffn2_w_out,
        "w_in_t": w_in_t, "w_out": w_out, "mla_w_uq": uq, "mla_w_ukv": mla_w_ukv,
        "norm_ffn1": r3(norm_ffn1), "norm_mix": r3(norm_mix), "norm_ffn2": r3(norm_ffn2),
        "gqa_q_norm": r3(gqa_q_norm), "gqa_k_norm": r3(gqa_k_norm),
        "mla_q_norm": r3(mla_q_norm), "mla_kv_norm": r3(mla_kv_norm),
        "diff_l": (r3(diff_lq1), r3(diff_lk1), r3(diff_lq2), r3(diff_lk2)), "diff_subln": r3(diff_subln),
        "dims": dims,
    }
    caches = {
        "diff_k": cache_diff_k.reshape(dec_batch, depth, past, -1),
        "diff_v": cache_diff_v.reshape(dec_batch, depth, past, -1),
        "gqa_k": cache_gqa_k.reshape(dec_batch, depth, past, -1),
        "gqa_v": cache_gqa_v.reshape(dec_batch, depth, past, -1),
        "mla_ckv": cache_mla_ckv, "mla_kpe": cache_kpe,
    }
    tables = (_rope_tables(dec_seq, HEAD_DIM), _rope_tables(dec_seq, DIFF_QK))

    h = x_prompt.reshape(batch * seq, d)
    state, kpes = None, []
    for l in range(depth):
        h, (state, kpe_l) = _layer(h, l, weights, mod4, (0, batch * seq), batch, seq, None, None, True, state)
        kpes.append(kpe_l.reshape(batch, seq, MLA_ROPE))
    y_prompt = _final_norm(h, final_norm).reshape(batch, seq, d)

    h = x_sample.reshape(dec_batch * dec_seq, d)
    for l in range(depth):
        h, _ = _layer(h, l, weights, mod4, (1, dec_seq), dec_batch, dec_seq, caches, tables, False)
    y_sample = _final_norm(h, final_norm).reshape(dec_batch, dec_seq, d)

    dk, dv, gk, gv, ckv = state
    lead = (batch, depth, seq)
    return (y_prompt, y_sample,
            dk.reshape(lead + (diff_heads, 2, DIFF_QK)), dv.reshape(lead + (diff_heads, DIFF_V)),
            gk.reshape(lead + (kv_heads, HEAD_DIM)), gv.reshape(lead + (kv_heads, HEAD_DIM)),
            ckv, jnp.stack(kpes, axis=1))
```
